```python
import math
import jax, jax.numpy as jnp
from jax import lax
import numpy as np

D_MODEL = 1024
BATCH = 8
SEQ = 2048
DEPTH = 4
DEC_BATCH = 128
DEC_SEQ = 4
PAST_LEN = 16384
PAGE_SIZE = 128

S5_WIDTH = D_MODEL
S5_GROUP = 16
S5_GROUPS = S5_WIDTH // S5_GROUP
S5_STATE = 64
S5_DT_MIN = 0.001
S5_DT_MAX = 0.1
M_EXPAND = 2
M_INNER = M_EXPAND * D_MODEL
M_HEADDIM = 64
M_HEADS = M_INNER // M_HEADDIM
M_GROUPS = 4
M_REP = M_HEADS // M_GROUPS
M_DSTATE = 128
M_CONV = 4
M_CONV_DIM = M_INNER + 2 * M_GROUPS * M_DSTATE
M_CHUNK = 128
M_DT_MIN = 0.001
M_DT_MAX = 0.1
E_GROUPS = 4
E_PER_GROUP = 8
N_EXPERTS = E_GROUPS * E_PER_GROUP
E_TOPK = 2
E_FF = 512
EPS = 1e-6
IN_COLS = S5_WIDTH + M_INNER + M_CONV_DIM + M_HEADS + 2 * D_MODEL
IN_SPLITS = (S5_WIDTH, S5_WIDTH + M_INNER, S5_WIDTH + M_INNER + M_CONV_DIM, S5_WIDTH + M_INNER + M_CONV_DIM + M_HEADS, S5_WIDTH + M_INNER + M_CONV_DIM + M_HEADS + D_MODEL)

kernel_name = "adaln_s5_ssd_hmoe_hybrid_step"

F32 = jnp.float32


def rmsnorm(x, w):
    x32 = x.astype(F32)
    y = x32 * lax.rsqrt(jnp.mean(x32 * x32, axis=-1, keepdims=True) + EPS)
    return (y * w.astype(F32)).astype(x.dtype)


def group_rmsnorm(y, w, groups):
    shp = y.shape
    y32 = y.astype(F32).reshape(shp[:-1] + (groups, shp[-1] // groups))
    y32 = y32 * lax.rsqrt(jnp.mean(y32 * y32, axis=-1, keepdims=True) + EPS)
    return y32.reshape(shp) * w.astype(F32)


def s5_branch(u, h0_re, h0_im, a_re, a_im, log_dt, b_re, b_im, c_re, c_im, d_skip, w_glu):
    bsz, L, _ = u.shape
    ug = u.astype(F32).reshape(bsz, L, S5_GROUPS, S5_GROUP)
    a = lax.complex(a_re.astype(F32), a_im.astype(F32))
    dt = jnp.exp(log_dt.astype(F32))[:, None]
    a_bar = jnp.exp(a * dt)
    b_bar = ((a_bar - 1.0) / a)[..., None] * lax.complex(b_re.astype(F32), b_im.astype(F32))
    bu = jnp.einsum('gnp,blgp->blgn', b_bar, ug.astype(jnp.complex64))
    h0 = lax.complex(h0_re.astype(F32), h0_im.astype(F32))
    bu = bu.at[:, 0].add(a_bar * h0)
    a_seq = jnp.broadcast_to(a_bar, (1, L) + a_bar.shape)

    def combine(left, right):
        a_l, b_l = left
        a_r, b_r = right
        return a_r * a_l, a_r * b_l + b_r

    _, h = lax.associative_scan(combine, (a_seq, bu), axis=1)
    c = lax.complex(c_re.astype(F32), c_im.astype(F32))
    y = jnp.real(jnp.einsum('gpn,blgn->blgp', c, h)) + d_skip.astype(F32).reshape(S5_GROUPS, S5_GROUP) * ug
    y = jax.nn.gelu(y.reshape(bsz, L, S5_WIDTH))
    y = y * jax.nn.sigmoid(y @ w_glu.astype(F32))
    h_last = h[:, -1]
    return y.astype(u.dtype), jnp.real(h_last).astype(h0_re.dtype), jnp.imag(h_last).astype(h0_re.dtype)


def ssd(x, dt, a, bm, cm, h0):
    bsz, L = x.shape[:2]
    q = min(M_CHUNK, L)
    pad = (-L) % q
    x, bm, cm = x.astype(F32), bm.astype(F32), cm.astype(F32)
    if pad:
        padw = lambda t: jnp.pad(t, [(0, 0), (0, pad)] + [(0, 0)] * (t.ndim - 2))
        x, dt, bm, cm = padw(x), padw(dt), padw(bm), padw(cm)
    nc = (L + pad) // q
    chunk = lambda t: t.reshape((bsz, nc, q) + t.shape[2:])
    xdt = chunk(x * dt[..., None])
    dtc, bc, cc = chunk(dt), chunk(bm), chunk(cm)
    a_cum = jnp.cumsum(dtc * a, axis=2)
    seg = a_cum[:, :, :, None] - a_cum[:, :, None, :]
    causal = jnp.tril(jnp.ones((q, q), dtype=bool))[:, :, None, None]
    lmat = jnp.exp(jnp.where(causal, seg, -jnp.inf))
    cb = jnp.einsum('bcign,bcjgn->bcijg', cc, bc)
    y_diag = jnp.einsum('bcijgr,bcjgrp->bcigrp', cb[..., None] * lmat, xdt)
    decay_to_end = jnp.exp(a_cum[:, :, -1:] - a_cum)
    chunk_states = jnp.einsum('bcjgn,bcjgrp->bcgrpn', bc, xdt * decay_to_end[..., None])
    chunk_decay = jnp.exp(a_cum[:, :, -1])

    def step(h, inp):
        dec, st = inp
        return dec[..., None, None] * h + st, h

    h_last, h_prev = lax.scan(step, h0.astype(F32), (jnp.moveaxis(chunk_decay, 1, 0), jnp.moveaxis(chunk_states, 1, 0)))
    h_prev = jnp.moveaxis(h_prev, 0, 1)
    y_off = jnp.einsum('bcign,bcgrpn->bcigrp', cc, h_prev) * jnp.exp(a_cum)[..., None]
    y = (y_diag + y_off).reshape((bsz, nc * q) + x.shape[2:])[:, :L]
    return y, h_last


def mamba_branch(z, xbc, dt_raw, h0, conv_buf, conv_w, conv_b, dt_bias, a_log, d_skip, norm_w):
    bsz, L, _ = xbc.shape
    xpad = jnp.concatenate([conv_buf.astype(xbc.dtype), xbc], axis=1)
    new_buf = xpad[:, L:]
    conv = lax.conv_general_dilated(xpad, conv_w.astype(xpad.dtype)[:, None, :], window_strides=(1,), padding='VALID', dimension_numbers=('NWC', 'WIO', 'NWC'), feature_group_count=M_CONV_DIM)
    xbc = jax.nn.silu(conv + conv_b)
    xs, bm, cm = jnp.split(xbc, [M_INNER, M_INNER + M_GROUPS * M_DSTATE], axis=-1)
    xs = xs.astype(F32).reshape(bsz, L, M_GROUPS, M_REP, M_HEADDIM)
    bm = bm.reshape(bsz, L, M_GROUPS, M_DSTATE)
    cm = cm.reshape(bsz, L, M_GROUPS, M_DSTATE)
    dt = jax.nn.softplus(dt_raw.astype(F32) + dt_bias.astype(F32)).reshape(bsz, L, M_GROUPS, M_REP)
    a = -jnp.exp(a_log.astype(F32)).reshape(M_GROUPS, M_REP)
    y, h_last = ssd(xs, dt, a, bm, cm, h0.reshape(bsz, M_GROUPS, M_REP, M_HEADDIM, M_DSTATE))
    y = y + d_skip.astype(F32).reshape(M_GROUPS, M_REP)[..., None] * xs
    y = y.reshape(bsz, L, M_INNER) * jax.nn.silu(z.astype(F32))
    y = group_rmsnorm(y, norm_w, M_GROUPS)
    return y.astype(z.dtype), h_last.reshape(bsz, M_HEADS, M_HEADDIM, M_DSTATE).astype(h0.dtype), new_buf.astype(conv_buf.dtype)


def moe(h, wr_g, br_g, wr_e, br_e, w_gate, w_up, w_down):
    bsz, L, d = h.shape
    t = h.reshape(bsz * L, d)
    g_logits = (t @ wr_g).astype(F32) + br_g.astype(F32)
    g_idx = jnp.argmax(g_logits, axis=-1)
    g_p = jnp.take_along_axis(jax.nn.softmax(g_logits, axis=-1), g_idx[:, None], axis=-1)
    e_logits = ((t @ wr_e).astype(F32) + br_e.astype(F32)).reshape(-1, E_GROUPS, E_PER_GROUP)
    e_in = jnp.take_along_axis(e_logits, g_idx[:, None, None], axis=1)[:, 0]
    top_v, top_i = lax.top_k(e_in, E_TOPK)
    top_w = jax.nn.softmax(top_v, axis=-1) * g_p
    exp_w = jnp.einsum('tk,tke->te', top_w, jax.nn.one_hot(top_i, E_PER_GROUP, dtype=F32))
    combine = (jax.nn.one_hot(g_idx, E_GROUPS, dtype=F32)[:, :, None] * exp_w[:, None, :]).astype(h.dtype)
    out = jnp.zeros_like(t)
    for gi in range(E_GROUPS):
        act = jax.nn.silu(jnp.einsum('td,edf->tef', t, w_gate[gi])) * jnp.einsum('td,edf->tef', t, w_up[gi])
        out = out + jnp.einsum('tef,efd->td', act * combine[:, gi, :, None], w_down[gi])
    return out.reshape(bsz, L, d)


def trunk(x, c, s5_re0, s5_im0, ssm0, conv0, norm1, norm2, w_mod, b_mod, w_in, s5_a_re, s5_a_im, s5_log_dt, s5_b_re, s5_b_im, s5_c_re, s5_c_im, s5_d, w_glu, conv_w, conv_b, dt_bias, a_log, m_d, m_norm, w_pa, w_pb, w_out, router_g, router_g_b, router_e, router_e_b, w_gate, w_up, w_down, norm_f):
    new_re, new_im, new_ssm, new_conv = [], [], [], []
    for l in range(DEPTH):
        mod = (jax.nn.silu(c) @ w_mod[l] + b_mod[l])[:, None, :]
        sh1, sc1, g1, sh2, sc2, g2 = jnp.split(mod, 6, axis=-1)
        h = rmsnorm(x, norm1[l]) * (1 + sc1) + sh1
        u, z, xbc, dt_raw, g_a, g_b = jnp.split(h @ w_in[l], IN_SPLITS, axis=-1)
        y_a, h_re, h_im = s5_branch(u, s5_re0[l], s5_im0[l], s5_a_re[l], s5_a_im[l], s5_log_dt[l], s5_b_re[l], s5_b_im[l], s5_c_re[l], s5_c_im[l], s5_d[l], w_glu[l])
        y_b, h_ssm, buf = mamba_branch(z, xbc, dt_raw, ssm0[l], conv0[l], conv_w[l], conv_b[l], dt_bias[l], a_log[l], m_d[l], m_norm[l])
        merged = jax.nn.sigmoid(g_a) * (y_a @ w_pa[l]) + jax.nn.sigmoid(g_b) * (y_b @ w_pb[l])
        x = x + g1 * (merged @ w_out[l])
        h2 = rmsnorm(x, norm2[l]) * (1 + sc2) + sh2
        x = x + g2 * moe(h2, router_g[l], router_g_b[l], router_e[l], router_e_b[l], w_gate[l], w_up[l], w_down[l])
        new_re.append(h_re)
        new_im.append(h_im)
        new_ssm.append(h_ssm)
        new_conv.append(buf)
    y = rmsnorm(x, norm_f)
    return y, jnp.stack(new_re), jnp.stack(new_im), jnp.stack(new_ssm), jnp.stack(new_conv)


def setup_inputs(seed: int = 0) -> dict:
    key = jax.random.key(seed)
    ks = iter(jax.random.split(key, 64))

    def nrm(shape, scale):
        return jax.random.normal(next(ks), shape, F32) * scale

    def unif(shape, lo, hi):
        return jax.random.uniform(next(ks), shape, F32, lo, hi)

    L = DEPTH
    d_inv = D_MODEL ** -0.5
    m_dt = jnp.exp(unif((L, M_HEADS), math.log(M_DT_MIN), math.log(M_DT_MAX)))
    return {
        'x_prompt': nrm((BATCH, SEQ, D_MODEL), 1.0),
        'x_sample': nrm((DEC_BATCH, DEC_SEQ, D_MODEL), 1.0),
        'state_s5_re': nrm((L, DEC_BATCH, S5_GROUPS, S5_STATE), 0.1),
        'state_s5_im': nrm((L, DEC_BATCH, S5_GROUPS, S5_STATE), 0.1),
        'state_ssm': nrm((L, DEC_BATCH, M_HEADS, M_HEADDIM, M_DSTATE), 0.1),
        'state_conv': nrm((L, DEC_BATCH, M_CONV - 1, M_CONV_DIM), 1.0),
        'c_prompt': nrm((BATCH, D_MODEL), 1.0),
        'c_sample': nrm((DEC_BATCH, D_MODEL), 1.0),
        'norm1': 1.0 + nrm((L, D_MODEL), 0.02),
        'norm2': 1.0 + nrm((L, D_MODEL), 0.02),
        'w_mod': nrm((L, D_MODEL, 6 * D_MODEL), 0.5 * d_inv),
        'b_mod': nrm((L, 6 * D_MODEL), 0.02),
        'w_in': nrm((L, D_MODEL, IN_COLS), d_inv),
        's5_a_re': -0.5 + nrm((L, S5_GROUPS, S5_STATE), 0.01),
        's5_a_im': math.pi * jnp.arange(S5_STATE, dtype=F32) + nrm((L, S5_GROUPS, S5_STATE), 0.01),
        's5_log_dt': unif((L, S5_GROUPS), math.log(S5_DT_MIN), math.log(S5_DT_MAX)),
        's5_b_re': nrm((L, S5_GROUPS, S5_STATE, S5_GROUP), (2 * S5_GROUP) ** -0.5),
        's5_b_im': nrm((L, S5_GROUPS, S5_STATE, S5_GROUP), (2 * S5_GROUP) ** -0.5),
        's5_c_re': nrm((L, S5_GROUPS, S5_GROUP, S5_STATE), 2.0 * S5_STATE ** -0.5),
        's5_c_im': nrm((L, S5_GROUPS, S5_GROUP, S5_STATE), 2.0 * S5_STATE ** -0.5),
        's5_d': nrm((L, S5_WIDTH), 1.0),
        'w_glu': nrm((L, S5_WIDTH, S5_WIDTH), S5_WIDTH ** -0.5),
        'conv_w': nrm((L, M_CONV, M_CONV_DIM), M_CONV ** -0.5),
        'conv_b': nrm((L, M_CONV_DIM), 0.02),
        'dt_bias': m_dt + jnp.log(-jnp.expm1(-m_dt)),
        'a_log': jnp.log(unif((L, M_HEADS), 1.0, 16.0)),
        'm_d': 1.0 + nrm((L, M_HEADS), 0.1),
        'm_norm': 1.0 + nrm((L, M_INNER), 0.02),
        'w_pa': nrm((L, S5_WIDTH, D_MODEL), S5_WIDTH ** -0.5),
        'w_pb': nrm((L, M_INNER, D_MODEL), M_INNER ** -0.5),
        'w_out': nrm((L, D_MODEL, D_MODEL), d_inv),
        'router_g': nrm((L, D_MODEL, E_GROUPS), d_inv),
        'router_g_b': nrm((L, E_GROUPS), 0.01),
        'router_e': nrm((L, D_MODEL, N_EXPERTS), d_inv),
        'router_e_b': nrm((L, N_EXPERTS), 0.01),
        'w_gate': nrm((L, E_GROUPS, E_PER_GROUP, D_MODEL, E_FF), d_inv),
        'w_up': nrm((L, E_GROUPS, E_PER_GROUP, D_MODEL, E_FF), d_inv),
        'w_down': nrm((L, E_GROUPS, E_PER_GROUP, E_FF, D_MODEL), E_FF ** -0.5),
        'norm_f': 1.0 + nrm((D_MODEL,), 0.02),
    }


def reference(x_prompt, x_sample, state_s5_re, state_s5_im, state_ssm, state_conv, c_prompt, c_sample, norm1, norm2, w_mod, b_mod, w_in, s5_a_re, s5_a_im, s5_log_dt, s5_b_re, s5_b_im, s5_c_re, s5_c_im, s5_d, w_glu, conv_w, conv_b, dt_bias, a_log, m_d, m_norm, w_pa, w_pb, w_out, router_g, router_g_b, router_e, router_e_b, w_gate, w_up, w_down, norm_f):
    weights = (norm1, norm2, w_mod, b_mod, w_in, s5_a_re, s5_a_im, s5_log_dt, s5_b_re, s5_b_im, s5_c_re, s5_c_im, s5_d, w_glu, conv_w, conv_b, dt_bias, a_log, m_d, m_norm, w_pa, w_pb, w_out, router_g, router_g_b, router_e, router_e_b, w_gate, w_up, w_down, norm_f)
    bsz = x_prompt.shape[0]
    z_re = jnp.zeros((DEPTH, bsz) + state_s5_re.shape[2:], state_s5_re.dtype)
    z_im = jnp.zeros((DEPTH, bsz) + state_s5_im.shape[2:], state_s5_im.dtype)
    z_ssm = jnp.zeros((DEPTH, bsz) + state_ssm.shape[2:], state_ssm.dtype)
    z_conv = jnp.zeros((DEPTH, bsz) + state_conv.shape[2:], state_conv.dtype)
    y_prompt, p_re, p_im, p_ssm, p_conv = trunk(x_prompt, c_prompt, z_re, z_im, z_ssm, z_conv, *weights)
    y_sample, s_re, s_im, s_ssm, s_conv = trunk(x_sample, c_sample, state_s5_re, state_s5_im, state_ssm, state_conv, *weights)
    return (y_prompt, y_sample, p_re, p_im, p_ssm, p_conv, s_re, s_im, s_ssm, s_conv)
```

```python
import functools
import math

import jax
import jax.numpy as jnp
from jax import lax
from jax.experimental import pallas as pl
from jax.experimental.pallas import tpu as pltpu

F32 = jnp.float32
BF16 = jnp.bfloat16
I32 = jnp.int32

D_MODEL = 1024
DEPTH = 4
S5_GROUP = 16
S5_GROUPS = 64
S5_STATE = 64
S5_COLBLOCKS = 8
S5_BLOCK_STATE = 512
M_INNER = 2048
M_HEADDIM = 64
M_HEADS = 32
M_GROUPS = 4
M_DSTATE = 128
M_CONV = 4
M_CONV_DIM = 3072
M_CHUNK = 128
E_GROUPS = 4
E_PER_GROUP = 8
N_EXPERTS = 32
E_FF = 512
EPS = 1e-6

LANES = 128
SUBLANES = 8
VMEM_LIMIT = 56 * 1024 * 1024
MAIN_COLS = 8192
COL_X, COL_Z, COL_BC, COL_U, COL_GA, COL_GB = 0, 2048, 4096, 5120, 6144, 7168
EXPERT_TILE = 256


def _cparams(n_axes, **kw):
    return pltpu.CompilerParams(dimension_semantics=("arbitrary",) * n_axes,
                                vmem_limit_bytes=VMEM_LIMIT, **kw)


def _dot(a, b):
    return jnp.dot(a, b, preferred_element_type=F32)


def _silu(x):
    return x / (1.0 + jnp.exp(-x))


def _sigmoid(x):
    return 1.0 / (1.0 + jnp.exp(-x))


def _gelu_tanh(x):
    return 0.5 * x * (1.0 + jnp.tanh(math.sqrt(2.0 / math.pi) * (x + 0.044715 * (x * x * x))))


def _softplus(x):
    return jnp.maximum(x, 0.0) + jnp.log(1.0 + jnp.exp(-jnp.abs(x)))


def _split3(v):
    hi = v.astype(BF16)
    r = v - hi.astype(F32)
    mid = r.astype(BF16)
    lo = (r - mid.astype(F32)).astype(BF16)
    return hi, mid, lo


def _mod_kernel(c_ref, w_ref, b_ref, o_ref):
    c = c_ref[...]
    o_ref[...] = _dot(_silu(c).astype(BF16), w_ref[...].astype(BF16)) + b_ref[...]


def _modulation(c_all, w_mod, b_mod):
    n = c_all.shape[0]
    return pl.pallas_call(
        _mod_kernel,
        grid=(DEPTH, 6),
        in_specs=[pl.BlockSpec((n, D_MODEL), lambda l, j: (0, 0)),
                  pl.BlockSpec((None, D_MODEL, D_MODEL), lambda l, j: (l, 0, j)),
                  pl.BlockSpec((None, 1, D_MODEL), lambda l, j: (l, 0, j))],
        out_specs=pl.BlockSpec((None, n, D_MODEL), lambda l, j: (l, 0, j)),
        out_shape=jax.ShapeDtypeStruct((DEPTH, n, 6 * D_MODEL), F32),
        compiler_params=_cparams(2),
        name="modulation",
    )(c_all, w_mod, b_mod.reshape(DEPTH, 1, 6 * D_MODEL))


def _s5_disc_kernel(are_ref, aim_ref, ldt_ref, bre_ref, bim_ref, abr_ref, abi_ref, bbr_ref, bbi_ref):
    a_re, a_im = are_ref[...], aim_ref[...]
    dt = jnp.exp(ldt_ref[...])
    mag = jnp.exp(a_re * dt)
    ab_re = mag * jnp.cos(a_im * dt)
    ab_im = mag * jnp.sin(a_im * dt)
    abr_ref[...] = ab_re
    abi_ref[...] = ab_im
    nr, ni = ab_re - 1.0, ab_im
    den = a_re * a_re + a_im * a_im
    co_re = (nr * a_re + ni * a_im) / den
    co_im = (ni * a_re - nr * a_im) / den
    b_re, b_im = bre_ref[...], bim_ref[...]
    cr, ci = co_re[:, None, :], co_im[:, None, :]
    bbr_ref[...] = cr * b_re - ci * b_im
    bbi_ref[...] = cr * b_im + ci * b_re


def _s5_discretise(a_re, a_im, log_dt, b_re, b_im):
    g, n, p = S5_GROUPS, S5_STATE, S5_GROUP
    bt_re = jnp.swapaxes(b_re, -1, -2)
    bt_im = jnp.swapaxes(b_im, -1, -2)
    s2 = pl.BlockSpec((None, g, n), lambda l: (l, 0, 0))
    s3 = pl.BlockSpec((None, g, p, n), lambda l: (l, 0, 0, 0))
    return pl.pallas_call(
        _s5_disc_kernel,
        grid=(DEPTH,),
        in_specs=[s2, s2, pl.BlockSpec((None, g, 1), lambda l: (l, 0, 0)), s3, s3],
        out_specs=[s2, s2, s3, s3],
        out_shape=[jax.ShapeDtypeStruct((DEPTH, g, n), F32)] * 2 + [jax.ShapeDtypeStruct((DEPTH, g, p, n), F32)] * 2,
        compiler_params=_cparams(1),
        name="s5_discretise",
    )(a_re, a_im, log_dt.reshape(DEPTH, g, 1), bt_re, bt_im)


def _s5_weights(ab_re, ab_im, bb_re, bb_im, c_re, c_im):
    cb, gl = S5_COLBLOCKS, S5_GROUPS // S5_COLBLOCKS
    eye = jnp.eye(gl, dtype=F32)

    def bdiag_in(bt):
        bt = bt.reshape(DEPTH, cb, gl, S5_GROUP, S5_STATE)
        return jnp.einsum('ljgpn,gh->ljgphn', bt, eye).reshape(DEPTH, cb, gl * S5_GROUP, gl * S5_STATE)

    def bdiag_out(c):
        c = c.reshape(DEPTH, cb, gl, S5_GROUP, S5_STATE)
        return jnp.einsum('ljgpn,gh->ljgnhp', c, eye).reshape(DEPTH, cb, gl * S5_STATE, gl * S5_GROUP)

    wb = jnp.concatenate([bdiag_in(bb_re), bdiag_in(bb_im)], axis=-1).astype(BF16)
    wc = jnp.concatenate([bdiag_out(c_re), -bdiag_out(c_im)], axis=-2).astype(BF16)
    ab = jnp.concatenate([ab_re.reshape(DEPTH, cb, 1, S5_BLOCK_STATE),
                          ab_im.reshape(DEPTH, cb, 1, S5_BLOCK_STATE)], axis=-1)
    return wb, wc, ab


def _in_proj_kernel(x_ref, nw_ref, sh_ref, sc_ref, w_ref, wdt_ref, main_ref, dt_ref, h_scr):
    @pl.when(pl.program_id(1) == 0)
    def _():
        x = x_ref[...]
        xn = x * lax.rsqrt(jnp.mean(x * x, axis=-1, keepdims=True) + EPS) * nw_ref[...]
        hb = (xn * (1.0 + sc_ref[...]) + sh_ref[...]).astype(BF16)
        h_scr[...] = hb
        dt_ref[...] = _dot(hb, wdt_ref[...])

    main_ref[...] = _dot(h_scr[...], w_ref[...])


def _in_proj(x, norm_w, modt, w_main, w_dt, tm):
    t = x.shape[0]
    tn = 1024
    return pl.pallas_call(
        _in_proj_kernel,
        grid=(t // tm, MAIN_COLS // tn),
        in_specs=[pl.BlockSpec((tm, D_MODEL), lambda i, j: (i, 0)),
                  pl.BlockSpec((1, D_MODEL), lambda i, j: (0, 0)),
                  pl.BlockSpec((None, tm, D_MODEL), lambda i, j: (0, 0, 0)),
                  pl.BlockSpec((None, tm, D_MODEL), lambda i, j: (1, 0, 0)),
                  pl.BlockSpec((D_MODEL, tn), lambda i, j: (0, j)),
                  pl.BlockSpec((D_MODEL, LANES), lambda i, j: (0, 0))],
        out_specs=[pl.BlockSpec((tm, tn), lambda i, j: (i, j)),
                   pl.BlockSpec((tm, LANES), lambda i, j: (i, 0))],
        out_shape=[jax.ShapeDtypeStruct((t, MAIN_COLS), F32), jax.ShapeDtypeStruct((t, LANES), F32)],
        scratch_shapes=[pltpu.VMEM((tm, D_MODEL), BF16)],
        compiler_params=_cparams(2),
        name="in_proj",
    )(x, norm_w, modt, modt, w_main, w_dt)


def _s5_kernel(u_ref, wb_ref, wc_ref, ab_ref, d_ref, h0_ref, y_ref, hl_ref, bu_scr, hs_scr, st_scr, *, nb, tq):
    ns = S5_BLOCK_STATE
    tt = pl.program_id(1)

    @pl.when(tt == 0)
    def _():
        st_scr[...] = h0_ref[...]

    u = u_ref[...]
    bu_scr[...] = _dot(u.astype(BF16), wb_ref[...])
    ar = jnp.broadcast_to(ab_ref[:, :ns], (nb, ns))
    ai = jnp.broadcast_to(ab_ref[:, ns:], (nb, ns))

    def step(s, carry):
        hr, hi = carry
        r0 = pl.multiple_of(s * nb, nb)
        nhr = ar * hr - ai * hi + bu_scr[pl.ds(r0, nb), :ns]
        nhi = ar * hi + ai * hr + bu_scr[pl.ds(r0, nb), ns:]
        hs_scr[pl.ds(r0, nb), :ns] = nhr
        hs_scr[pl.ds(r0, nb), ns:] = nhi
        return nhr, nhi

    hr, hi = lax.fori_loop(0, tq, step, (st_scr[:, :ns], st_scr[:, ns:]), unroll=min(tq, 8))
    st_scr[:, :ns] = hr
    st_scr[:, ns:] = hi
    y = _dot(hs_scr[...].astype(BF16), wc_ref[...]) + d_ref[...] * u
    y_ref[...] = _gelu_tanh(y)

    @pl.when(tt == pl.num_programs(1) - 1)
    def _():
        hl_ref[...] = st_scr[...]


def _s5_branch(main, wb, wc, ab, d_skip, h0, nb, tq):
    t = main.shape[0]
    rows = tq * nb
    cb = S5_COLBLOCKS
    ucol = COL_U // LANES
    return pl.pallas_call(
        functools.partial(_s5_kernel, nb=nb, tq=tq),
        grid=(cb, t // rows),
        in_specs=[pl.BlockSpec((rows, LANES), lambda j, i: (i, ucol + j)),
                  pl.BlockSpec((None, LANES, 2 * S5_BLOCK_STATE), lambda j, i: (j, 0, 0)),
                  pl.BlockSpec((None, 2 * S5_BLOCK_STATE, LANES), lambda j, i: (j, 0, 0)),
                  pl.BlockSpec((None, 1, 2 * S5_BLOCK_STATE), lambda j, i: (j, 0, 0)),
                  pl.BlockSpec((1, LANES), lambda j, i: (0, j)),
                  pl.BlockSpec((None, nb, 2 * S5_BLOCK_STATE), lambda j, i: (j, 0, 0))],
        out_specs=[pl.BlockSpec((rows, LANES), lambda j, i: (i, j)),
                   pl.BlockSpec((None, nb, 2 * S5_BLOCK_STATE), lambda j, i: (j, 0, 0))],
        out_shape=[jax.ShapeDtypeStruct((t, D_MODEL), F32),
                   jax.ShapeDtypeStruct((cb, nb, 2 * S5_BLOCK_STATE), F32)],
        scratch_shapes=[pltpu.VMEM((rows, 2 * S5_BLOCK_STATE), F32),
                        pltpu.VMEM((rows, 2 * S5_BLOCK_STATE), F32),
                        pltpu.VMEM((nb, 2 * S5_BLOCK_STATE), F32)],
        compiler_params=_cparams(2),
        name="s5_branch",
    )(main, wb, wc, ab, d_skip, h0)


def _ssd_kernel(x_ref, z_ref, bc_ref, dt_ref, conv0_ref, h0_ref, cw_ref, cb_ref, dtb_ref, alog_ref,
                dsk_ref, nw_ref, tril_ref, y_ref, hout_ref, convout_ref,
                xp_scr, xbc_scr, dt_scr, y_scr, xw_scr, st_scr, *, lr):
    q = M_CHUNK
    c = pl.program_id(1)
    last = pl.num_programs(1) - 1
    hp = M_HEADDIM
    gw = M_INNER // M_GROUPS

    @pl.when(c == 0)
    def _():
        st_scr[...] = h0_ref[...]
        xp_scr[0:SUBLANES, :] = conv0_ref[...]
        if lr < q:
            xp_scr[SUBLANES:, :] = jnp.zeros((q, M_CONV_DIM), F32)
            dt_scr[...] = jnp.zeros((q, LANES), F32)

    xp_scr[SUBLANES:SUBLANES + lr, 0:M_INNER] = x_ref[...]
    xp_scr[SUBLANES:SUBLANES + lr, M_INNER:M_CONV_DIM] = bc_ref[...]
    acc = cb_ref[...] + cw_ref[0:1, :] * xp_scr[5:5 + q, :]
    for k in range(1, M_CONV):
        acc = acc + cw_ref[k:k + 1, :] * xp_scr[5 + k:5 + k + q, :]
    xbc_scr[...] = _silu(acc)
    xbc = xbc_scr

    win = ((lr + 5) // SUBLANES) * SUBLANES

    @pl.when(c == last)
    def _():
        convout_ref[...] = xp_scr[win:win + SUBLANES, :]

    if lr == q:
        xp_scr[0:SUBLANES, :] = xp_scr[q:q + SUBLANES, :]

    dt_scr[0:lr, :] = _softplus(dt_ref[...] + dtb_ref[...])
    dt = dt_scr[...]
    da = dt * (-jnp.exp(alog_ref[...]))
    tril = tril_ref[...]
    p3 = _split3(da)
    acum = _dot(tril, p3[0]) + _dot(tril, p3[1]) + _dot(tril, p3[2])
    acum_t = acum.T

    row_i = lax.broadcasted_iota(I32, (q, q), 0)
    col_j = lax.broadcasted_iota(I32, (q, q), 1)
    causal = row_i >= col_j
    low_half = lax.broadcasted_iota(I32, (q, LANES), 1) < hp

    for g in range(M_GROUPS):
        bm = xbc[:, M_INNER + g * M_DSTATE:M_INNER + (g + 1) * M_DSTATE].astype(BF16)
        cm = xbc[:, M_INNER + M_GROUPS * M_DSTATE + g * M_DSTATE:
                 M_INNER + M_GROUPS * M_DSTATE + (g + 1) * M_DSTATE].astype(BF16)
        cbm = lax.dot_general(cm, bm, (((1,), (1,)), ((), ())), preferred_element_type=F32)
        st_g = st_scr[g * gw:(g + 1) * gw, :]
        y_off = lax.dot_general(cm, st_g.astype(BF16), (((1,), (1,)), ((), ())),
                                preferred_element_type=F32)
        for kk in range(gw // LANES):
            k = g * (gw // LANES) + kk
            cols = slice(k * LANES, (k + 1) * LANES)
            h_a, h_b = 2 * k, 2 * k + 1
            col_a = jnp.broadcast_to(acum[:, h_a:h_a + 1], (q, LANES))
            col_b = jnp.broadcast_to(acum[:, h_b:h_b + 1], (q, LANES))
            dt_p = jnp.where(low_half, jnp.broadcast_to(dt[:, h_a:h_a + 1], (q, LANES)),
                             jnp.broadcast_to(dt[:, h_b:h_b + 1], (q, LANES)))
            ac_p = jnp.where(low_half, col_a, col_b)
            end_p = jnp.broadcast_to(ac_p[q - 1:q, :], (q, LANES))
            xs_p = xbc[:, cols]
            xdt = xs_p * dt_p
            xdt_b = xdt.astype(BF16)
            l_a = jnp.where(causal, jnp.exp(jnp.minimum(col_a - acum_t[h_a:h_a + 1, :], 0.0)), 0.0)
            l_b = jnp.where(causal, jnp.exp(jnp.minimum(col_b - acum_t[h_b:h_b + 1, :], 0.0)), 0.0)
            r_a = _dot((cbm * l_a).astype(BF16), xdt_b)
            r_b = _dot((cbm * l_b).astype(BF16), xdt_b)
            y_scr[:, cols] = (jnp.where(low_half, r_a, r_b) + y_off[:, kk * LANES:(kk + 1) * LANES] * jnp.exp(ac_p)
                              + dsk_ref[:, cols] * xs_p)
            xw_scr[:, cols] = xdt * jnp.exp(end_p - ac_p)
            dec = jnp.exp(ac_p[q - 1:q, :])
            dec_a = jnp.broadcast_to(dec[:, 0:1], (hp, M_DSTATE))
            dec_b = jnp.broadcast_to(dec[:, hp:hp + 1], (hp, M_DSTATE))
            st_scr[h_a * hp:(h_a + 1) * hp, :] = st_scr[h_a * hp:(h_a + 1) * hp, :] * dec_a
            st_scr[h_b * hp:(h_b + 1) * hp, :] = st_scr[h_b * hp:(h_b + 1) * hp, :] * dec_b
        xw_g = xw_scr[:, g * gw:(g + 1) * gw].astype(BF16)
        st_scr[g * gw:(g + 1) * gw, :] += lax.dot_general(xw_g, bm, (((0,), (0,)), ((), ())),
                                                          preferred_element_type=F32)

    yg = y_scr[0:lr, :] * _silu(z_ref[...])
    for g in range(M_GROUPS):
        v = yg[:, g * gw:(g + 1) * gw]
        y_ref[:, g * gw:(g + 1) * gw] = (v * lax.rsqrt(jnp.mean(v * v, axis=-1, keepdims=True) + EPS)
                                         * nw_ref[:, g * gw:(g + 1) * gw])

    @pl.when(c == last)
    def _():
        hout_ref[...] = st_scr[...]


def _ssd_branch(main, dt_raw, conv0, h0, cw, cb, dtb, alog, dsk, nw, tril, nb, seq):
    q = M_CHUNK
    lr = min(q, seq)
    nc = seq // lr
    main_v = main.reshape(seq, nb * MAIN_COLS)
    dt_v = dt_raw.reshape(seq, nb * LANES)
    wide, narrow = MAIN_COLS // M_INNER, MAIN_COLS // 1024
    const = lambda shape: pl.BlockSpec(shape, lambda b, c: (0,) * len(shape))
    y, hout, convout = pl.pallas_call(
        functools.partial(_ssd_kernel, lr=lr),
        grid=(nb, nc),
        in_specs=[pl.BlockSpec((lr, M_INNER), lambda b, c: (c, wide * b + COL_X // M_INNER)),
                  pl.BlockSpec((lr, M_INNER), lambda b, c: (c, wide * b + COL_Z // M_INNER)),
                  pl.BlockSpec((lr, 1024), lambda b, c: (c, narrow * b + COL_BC // 1024)),
                  pl.BlockSpec((lr, LANES), lambda b, c: (c, b)),
                  pl.BlockSpec((None, SUBLANES, M_CONV_DIM), lambda b, c: (b, 0, 0)),
                  pl.BlockSpec((None, M_INNER, M_DSTATE), lambda b, c: (b, 0, 0)),
                  const((SUBLANES, M_CONV_DIM)), const((1, M_CONV_DIM)), const((1, LANES)), const((1, LANES)),
                  const((1, M_INNER)), const((1, M_INNER)), const((q, q))],
        out_specs=[pl.BlockSpec((lr, M_INNER), lambda b, c: (c, b)),
                   pl.BlockSpec((None, M_INNER, M_DSTATE), lambda b, c: (b, 0, 0)),
                   pl.BlockSpec((None, SUBLANES, M_CONV_DIM), lambda b, c: (b, 0, 0))],
        out_shape=[jax.ShapeDtypeStruct((seq, nb * M_INNER), F32),
                   jax.ShapeDtypeStruct((nb, M_INNER, M_DSTATE), F32),
                   jax.ShapeDtypeStruct((nb, SUBLANES, M_CONV_DIM), F32)],
        scratch_shapes=[pltpu.VMEM((q + SUBLANES, M_CONV_DIM), F32),
                        pltpu.VMEM((q, M_CONV_DIM), F32),
                        pltpu.VMEM((q, LANES), F32),
                        pltpu.VMEM((q, M_INNER), F32),
                        pltpu.VMEM((q, M_INNER), F32),
                        pltpu.VMEM((M_INNER, M_DSTATE), F32)],
        compiler_params=_cparams(2),
        name="ssd_branch",
    )(main_v, main_v, main_v, dt_v, conv0, h0, cw, cb, dtb, alog, dsk, nw, tril)
    off = lr + 5 - ((lr + 5) // SUBLANES) * SUBLANES
    return y.reshape(seq * nb, M_INNER), hout, convout[:, off:off + M_CONV - 1]


def _route(logits, cnt_scr, ltri):
    tm = logits.shape[0]
    lane = lax.broadcasted_iota(I32, (tm, LANES), 1).astype(F32)
    neg = jnp.float32(-jnp.inf)
    far = jnp.float32(LANES)
    glog = jnp.where(lane < E_GROUPS, logits, neg)
    gmax = jnp.max(glog, axis=-1, keepdims=True)
    g_idx = jnp.min(jnp.where(glog == gmax, lane, far), axis=-1, keepdims=True)
    g_p = 1.0 / jnp.sum(jnp.exp(glog - gmax), axis=-1, keepdims=True)
    e_lane = lane - E_GROUPS
    in_group = (e_lane >= g_idx * E_PER_GROUP) & (e_lane < (g_idx + 1.0) * E_PER_GROUP)
    elog = jnp.where(in_group, logits, neg)
    v0 = jnp.max(elog, axis=-1, keepdims=True)
    i0 = jnp.min(jnp.where(elog == v0, lane, far), axis=-1, keepdims=True)
    elog1 = jnp.where(lane == i0, neg, elog)
    v1 = jnp.max(elog1, axis=-1, keepdims=True)
    i1 = jnp.min(jnp.where(elog1 == v1, lane, far), axis=-1, keepdims=True)
    e1x = jnp.exp(v1 - v0)
    w0 = g_p / (1.0 + e1x)
    w1 = g_p * e1x / (1.0 + e1x)
    e0, e1 = i0 - E_GROUPS, i1 - E_GROUPS
    hit0, hit1 = lane == e0, lane == e1
    onehot = jnp.where(hit0 | hit1, 1.0, 0.0)
    before = _dot(ltri, onehot.astype(BF16)) + cnt_scr[...]
    rank0 = jnp.sum(jnp.where(hit0, before, 0.0), axis=-1, keepdims=True)
    rank1 = jnp.sum(jnp.where(hit1, before, 0.0), axis=-1, keepdims=True)
    cnt_scr[...] += jnp.sum(onehot, axis=0, keepdims=True)
    ri = jnp.where(lane == 0, e0, jnp.where(lane == 1, e1, jnp.where(lane == 2, rank0, jnp.where(lane == 3, rank1, 0.0))))
    rw = jnp.where(lane == 0, w0, jnp.where(lane == 1, w1, 0.0))
    return ri.astype(I32), rw


def _merge_kernel(ya_ref, yb_ref, ga_ref, gb_ref, x_ref, g1_ref, sh2_ref, sc2_ref, n2_ref,
                  wglu_ref, wpa_ref, wpb_ref, wout_ref, wr_ref, br_ref, ltri_ref,
                  xo_ref, h2_ref, ri_ref, rw_ref, cnt_ref, cnt_scr):
    i = pl.program_id(0)

    @pl.when(i == 0)
    def _():
        cnt_scr[...] = jnp.zeros_like(cnt_scr)

    ya = ya_ref[...]
    ya = ya * _sigmoid(_dot(ya.astype(BF16), wglu_ref[...]))
    pa = _dot(ya.astype(BF16), wpa_ref[...])
    pb = _dot(yb_ref[...].astype(BF16), wpb_ref[...])
    merged = _sigmoid(ga_ref[...]) * pa + _sigmoid(gb_ref[...]) * pb
    x = x_ref[...] + g1_ref[...] * _dot(merged.astype(BF16), wout_ref[...])
    xo_ref[...] = x
    xn = x * lax.rsqrt(jnp.mean(x * x, axis=-1, keepdims=True) + EPS) * n2_ref[...]
    h2 = xn * (1.0 + sc2_ref[...]) + sh2_ref[...]
    h2_ref[...] = h2
    logits = _dot(h2.astype(BF16), wr_ref[...]) + br_ref[...]
    ri, rw = _route(logits, cnt_scr, ltri_ref[...])
    ri_ref[...] = ri
    rw_ref[...] = rw

    @pl.when(i == pl.num_programs(0) - 1)
    def _():
        cnt_ref[...] = cnt_scr[...]


def _merge(ya, yb, main, x, modt, n2, wglu, wpa, wpb, wout, wr, br, ltri, tm):
    t = x.shape[0]
    row = lambda w: pl.BlockSpec((tm, w), lambda i: (i, 0))
    const = lambda shape: pl.BlockSpec(shape, lambda i: (0,) * len(shape))
    modspec = lambda k: pl.BlockSpec((None, tm, D_MODEL), lambda i: (k, 0, 0))
    return pl.pallas_call(
        _merge_kernel,
        grid=(t // tm,),
        in_specs=[row(D_MODEL), row(M_INNER),
                  pl.BlockSpec((tm, D_MODEL), lambda i: (i, COL_GA // D_MODEL)),
                  pl.BlockSpec((tm, D_MODEL), lambda i: (i, COL_GB // D_MODEL)),
                  row(D_MODEL), modspec(2), modspec(3), modspec(4), const((1, D_MODEL)),
                  const((D_MODEL, D_MODEL)), const((D_MODEL, D_MODEL)), const((M_INNER, D_MODEL)),
                  const((D_MODEL, D_MODEL)), const((D_MODEL, LANES)), const((1, LANES)), const((tm, tm))],
        out_specs=[row(D_MODEL), row(D_MODEL), row(LANES), row(LANES), const((1, LANES))],
        out_shape=[jax.ShapeDtypeStruct((t, D_MODEL), F32), jax.ShapeDtypeStruct((t, D_MODEL), F32),
                   jax.ShapeDtypeStruct((t, LANES), I32), jax.ShapeDtypeStruct((t, LANES), F32),
                   jax.ShapeDtypeStruct((1, LANES), F32)],
        scratch_shapes=[pltpu.VMEM((1, LANES), F32)],
        compiler_params=_cparams(1),
        name="merge_route",
    )(ya, yb, main, main, x, modt, modt, modt, n2, wglu, wpa, wpb, wout, wr, br, ltri)


def _row_copy(src_ref, src_row, dst_ref, dst_row, sem):
    return pltpu.make_async_copy(src_ref.at[pl.ds(src_row, 1)], dst_ref.at[pl.ds(dst_row, 1)], sem)


def _dispatch_kernel(p0_ref, p1_ref, h2_ref, zeros_ref, xs_ref, sem, *, tm):
    del zeros_ref
    base = pl.program_id(0) * tm

    def issue(r, carry):
        _row_copy(h2_ref, r, xs_ref, p0_ref[base + r], sem).start()
        _row_copy(h2_ref, r, xs_ref, p1_ref[base + r], sem).start()
        return carry

    lax.fori_loop(0, tm, issue, 0)

    def drain(r, carry):
        _row_copy(h2_ref, r, xs_ref, p0_ref[base + r], sem).wait()
        _row_copy(h2_ref, r, xs_ref, p1_ref[base + r], sem).wait()
        return carry

    lax.fori_loop(0, tm, drain, 0)


def _dispatch(h2, pos0, pos1, npad, tm):
    t = h2.shape[0]
    zeros = jnp.zeros((npad, D_MODEL), F32)
    return pl.pallas_call(
        functools.partial(_dispatch_kernel, tm=tm),
        grid_spec=pltpu.PrefetchScalarGridSpec(
            num_scalar_prefetch=2,
            grid=(t // tm,),
            in_specs=[pl.BlockSpec((tm, D_MODEL), lambda i, p0, p1: (i, 0)),
                      pl.BlockSpec(memory_space=pl.ANY)],
            out_specs=pl.BlockSpec(memory_space=pl.ANY),
            scratch_shapes=[pltpu.SemaphoreType.DMA(())]),
        out_shape=jax.ShapeDtypeStruct((npad, D_MODEL), F32),
        input_output_aliases={3: 0},
        compiler_params=_cparams(1),
        name="moe_dispatch",
    )(pos0, pos1, h2, zeros)


def _expert_kernel(te_ref, tv_ref, xs_ref, wg_ref, wu_ref, wd_ref, o_ref):
    i = pl.program_id(0)

    @pl.when(tv_ref[i] > 0)
    def _():
        xb = xs_ref[...].astype(BF16)
        act = _silu(_dot(xb, wg_ref[...])) * _dot(xb, wu_ref[...])
        o_ref[...] = _dot(act.astype(BF16), wd_ref[...])

    @pl.when(tv_ref[i] == 0)
    def _():
        o_ref[...] = jnp.zeros_like(o_ref)


def _experts(xs, tile_expert, tile_valid, wg, wu, wd):
    npad = xs.shape[0]
    te = EXPERT_TILE
    return pl.pallas_call(
        _expert_kernel,
        grid_spec=pltpu.PrefetchScalarGridSpec(
            num_scalar_prefetch=2,
            grid=(npad // te,),
            in_specs=[pl.BlockSpec((te, D_MODEL), lambda i, e, v: (i, 0)),
                      pl.BlockSpec((None, D_MODEL, E_FF), lambda i, e, v: (e[i], 0, 0)),
                      pl.BlockSpec((None, D_MODEL, E_FF), lambda i, e, v: (e[i], 0, 0)),
                      pl.BlockSpec((None, E_FF, D_MODEL), lambda i, e, v: (e[i], 0, 0))],
            out_specs=pl.BlockSpec((te, D_MODEL), lambda i, e, v: (i, 0))),
        out_shape=jax.ShapeDtypeStruct((npad, D_MODEL), F32),
        compiler_params=_cparams(1),
        name="moe_experts",
    )(tile_expert, tile_valid, xs, wg, wu, wd)


def _combine_kernel(p0_ref, p1_ref, ys_ref, x_ref, rw_ref, g2_ref, nf_ref, o_ref, b0, b1, sem, *, tm, final):
    base = pl.program_id(0) * tm

    def issue(r, carry):
        _row_copy(ys_ref, p0_ref[base + r], b0, r, sem).start()
        _row_copy(ys_ref, p1_ref[base + r], b1, r, sem).start()
        return carry

    lax.fori_loop(0, tm, issue, 0)

    def drain(r, carry):
        _row_copy(ys_ref, p0_ref[base + r], b0, r, sem).wait()
        _row_copy(ys_ref, p1_ref[base + r], b1, r, sem).wait()
        return carry

    lax.fori_loop(0, tm, drain, 0)
    rw = rw_ref[...]
    w0 = jnp.broadcast_to(rw[:, 0:1], (tm, D_MODEL))
    w1 = jnp.broadcast_to(rw[:, 1:2], (tm, D_MODEL))
    x = x_ref[...] + g2_ref[...] * (w0 * b0[...] + w1 * b1[...])
    if final:
        x = x * lax.rsqrt(jnp.mean(x * x, axis=-1, keepdims=True) + EPS) * nf_ref[...]
    o_ref[...] = x


def _combine(ys, pos0, pos1, x, rw, modt, norm_f, tm, final):
    t = x.shape[0]
    row = lambda w: pl.BlockSpec((tm, w), lambda i, p0, p1: (i, 0))
    return pl.pallas_call(
        functools.partial(_combine_kernel, tm=tm, final=final),
        grid_spec=pltpu.PrefetchScalarGridSpec(
            num_scalar_prefetch=2,
            grid=(t // tm,),
            in_specs=[pl.BlockSpec(memory_space=pl.ANY), row(D_MODEL), row(LANES),
                      pl.BlockSpec((None, tm, D_MODEL), lambda i, p0, p1: (5, 0, 0)),
                      pl.BlockSpec((1, D_MODEL), lambda i, p0, p1: (0, 0))],
            out_specs=row(D_MODEL),
            scratch_shapes=[pltpu.VMEM((tm, D_MODEL), F32), pltpu.VMEM((tm, D_MODEL), F32),
                            pltpu.SemaphoreType.DMA(())]),
        out_shape=jax.ShapeDtypeStruct((t, D_MODEL), F32),
        compiler_params=_cparams(1),
        name="moe_combine",
    )(pos0, pos1, ys, x, rw, modt, norm_f)


def _moe_plan(ri, counts, t):
    te = EXPERT_TILE
    npad = 2 * t + N_EXPERTS * te
    cnt = counts[0, :N_EXPERTS].astype(I32)
    padded = ((cnt + te - 1) // te) * te
    ends = jnp.cumsum(padded)
    offs = ends - padded
    pos0 = offs[ri[:, 0]] + ri[:, 2]
    pos1 = offs[ri[:, 1]] + ri[:, 3]
    starts = jnp.arange(npad // te, dtype=I32) * te
    tile_expert = jnp.minimum(jnp.searchsorted(ends, starts, side='right'), N_EXPERTS - 1).astype(I32)
    tile_valid = (starts < ends[-1]).astype(I32)
    return pos0.astype(I32), pos1.astype(I32), tile_expert, tile_valid, npad


def _trunk(x, mod, s5_re0, s5_im0, ssm0, conv0, p):
    nb, seq, _ = x.shape
    t = nb * seq
    tm = min(512, t)
    tm_merge = min(256, t)
    rows = min(512, t)
    tq = rows // nb
    cb = S5_COLBLOCKS
    xt = jnp.swapaxes(x, 0, 1).reshape(t, D_MODEL)
    modt = jnp.tile(mod.reshape(DEPTH, nb, 6, D_MODEL).transpose(0, 2, 1, 3), (1, 1, tm // nb, 1))
    new_re, new_im, new_ssm, new_conv = [], [], [], []
    yn = None
    for l in range(DEPTH):
        main, dt_raw = _in_proj(xt, p['norm1'][l], modt[l], p['w_main'][l], p['w_dt'][l], tm)
        h0 = jnp.concatenate([s5_re0[l].reshape(nb, cb, S5_BLOCK_STATE), s5_im0[l].reshape(nb, cb, S5_BLOCK_STATE)],
                             axis=-1).transpose(1, 0, 2)
        ya, hl = _s5_branch(main, p['wb'][l], p['wc'][l], p['ab'][l], p['s5_d'][l], h0, nb, tq)
        hl = hl.transpose(1, 0, 2)
        new_re.append(hl[..., :S5_BLOCK_STATE].reshape(nb, S5_GROUPS, S5_STATE))
        new_im.append(hl[..., S5_BLOCK_STATE:].reshape(nb, S5_GROUPS, S5_STATE))
        conv_in = jnp.pad(conv0[l], ((0, 0), (SUBLANES - (M_CONV - 1), 0), (0, 0)))
        yb, h_ssm, cbuf = _ssd_branch(main, dt_raw, conv_in, ssm0[l].reshape(nb, M_INNER, M_DSTATE),
                                      p['conv_w'][l], p['conv_b'][l], p['dt_bias'][l], p['a_log'][l],
                                      p['m_d'][l], p['m_norm'][l], p['tril'], nb, seq)
        new_ssm.append(h_ssm.reshape(nb, M_HEADS, M_HEADDIM, M_DSTATE))
        new_conv.append(cbuf)
        xt, h2, ri, rw, counts = _merge(ya, yb, main, xt, modt[l][:, :tm_merge], p['norm2'][l], p['w_glu'][l],
                                        p['w_pa'][l], p['w_pb'][l], p['w_out'][l], p['w_r'][l], p['b_r'][l],
                                        p['ltri'][:tm_merge, :tm_merge], tm_merge)
        pos0, pos1, tile_expert, tile_valid, npad = _moe_plan(ri, counts, t)
        xs = _dispatch(h2, pos0, pos1, npad, tm)
        ys = _experts(xs, tile_expert, tile_valid, p['w_gate'][l], p['w_up'][l], p['w_down'][l])
        xt = _combine(ys, pos0, pos1, xt, rw, modt[l], p['norm_f'], tm, l == DEPTH - 1)
    y = jnp.swapaxes(xt.reshape(seq, nb, D_MODEL), 0, 1)
    return y, jnp.stack(new_re), jnp.stack(new_im), jnp.stack(new_ssm), jnp.stack(new_conv)


def _prepare(norm1, norm2, w_in, s5_a_re, s5_a_im, s5_log_dt, s5_b_re, s5_b_im, s5_c_re, s5_c_im, s5_d, w_glu,
             conv_w, conv_b, dt_bias, a_log, m_d, m_norm, w_pa, w_pb, w_out, router_g, router_g_b, router_e,
             router_e_b, w_gate, w_up, w_down, norm_f):
    c_u, c_z, c_xbc, c_dt, c_ga = 1024, 1024 + 2048, 1024 + 2048 + 3072, 1024 + 2048 + 3072 + 32, 1024 + 2048 + 3072 + 32 + 1024
    w_u, w_z, w_xbc = w_in[..., :c_u], w_in[..., c_u:c_z], w_in[..., c_z:c_xbc]
    w_dtc, w_ga, w_gb = w_in[..., c_xbc:c_dt], w_in[..., c_dt:c_ga], w_in[..., c_ga:]
    w_main = jnp.concatenate([w_xbc[..., :M_INNER], w_z, w_xbc[..., M_INNER:], w_u, w_ga, w_gb], axis=-1).astype(BF16)
    w_dt = jnp.pad(w_dtc, ((0, 0), (0, 0), (0, LANES - M_HEADS))).astype(BF16)
    ab_re, ab_im, bb_re, bb_im = _s5_discretise(s5_a_re, s5_a_im, s5_log_dt, s5_b_re, s5_b_im)
    wb, wc, ab = _s5_weights(ab_re, ab_im, bb_re, bb_im, s5_c_re, s5_c_im)
    pad_heads = lambda v: jnp.pad(v, ((0, 0), (0, LANES - M_HEADS))).reshape(DEPTH, 1, LANES)
    w_r = jnp.pad(jnp.concatenate([router_g, router_e], axis=-1),
                  ((0, 0), (0, 0), (0, LANES - E_GROUPS - N_EXPERTS))).astype(BF16)
    b_r = jnp.pad(jnp.concatenate([router_g_b, router_e_b], axis=-1),
                  ((0, 0), (0, LANES - E_GROUPS - N_EXPERTS))).reshape(DEPTH, 1, LANES)
    idx = jnp.arange(512)
    return dict(
        norm1=norm1.reshape(DEPTH, 1, D_MODEL), norm2=norm2.reshape(DEPTH, 1, D_MODEL),
        norm_f=norm_f.reshape(1, D_MODEL), w_main=w_main, w_dt=w_dt, wb=wb, wc=wc, ab=ab,
        s5_d=s5_d.reshape(DEPTH, 1, D_MODEL), w_glu=w_glu.astype(BF16),
        conv_w=jnp.pad(conv_w, ((0, 0), (0, SUBLANES - M_CONV), (0, 0))),
        conv_b=conv_b.reshape(DEPTH, 1, M_CONV_DIM), dt_bias=pad_heads(dt_bias), a_log=pad_heads(a_log),
        m_d=jnp.repeat(m_d, M_HEADDIM, axis=-1).reshape(DEPTH, 1, M_INNER), m_norm=m_norm.reshape(DEPTH, 1, M_INNER),
        w_pa=w_pa.astype(BF16), w_pb=w_pb.astype(BF16), w_out=w_out.astype(BF16), w_r=w_r, b_r=b_r,
        w_gate=w_gate.reshape(DEPTH, N_EXPERTS, D_MODEL, E_FF).astype(BF16),
        w_up=w_up.reshape(DEPTH, N_EXPERTS, D_MODEL, E_FF).astype(BF16),
        w_down=w_down.reshape(DEPTH, N_EXPERTS, E_FF, D_MODEL).astype(BF16),
        tril=(idx[:M_CHUNK, None] >= idx[None, :M_CHUNK]).astype(BF16),
        ltri=(idx[:, None] > idx[None, :]).astype(BF16),
    )


def kernel(x_prompt, x_sample, state_s5_re, state_s5_im, state_ssm, state_conv, c_prompt, c_sample, norm1, norm2, w_mod, b_mod, w_in, s5_a_re, s5_a_im, s5_log_dt, s5_b_re, s5_b_im, s5_c_re, s5_c_im, s5_d, w_glu, conv_w, conv_b, dt_bias, a_log, m_d, m_norm, w_pa, w_pb, w_out, router_g, router_g_b, router_e, router_e_b, w_gate, w_up, w_down, norm_f):
    p = _prepare(norm1, norm2, w_in, s5_a_re, s5_a_im, s5_log_dt, s5_b_re, s5_b_im, s5_c_re, s5_c_im, s5_d, w_glu,
                 conv_w, conv_b, dt_bias, a_log, m_d, m_norm, w_pa, w_pb, w_out, router_g, router_g_b, router_e,
                 router_e_b, w_gate, w_up, w_down, norm_f)
    bp = x_prompt.shape[0]
    mod = _modulation(jnp.concatenate([c_prompt, c_sample], axis=0), w_mod, b_mod)
    zeros_like_state = lambda s: jnp.zeros((DEPTH, bp) + s.shape[2:], s.dtype)
    y_p, p_re, p_im, p_ssm, p_conv = _trunk(x_prompt, mod[:, :bp], zeros_like_state(state_s5_re),
                                            zeros_like_state(state_s5_im), zeros_like_state(state_ssm),
                                            zeros_like_state(state_conv), p)
    y_s, s_re, s_im, s_ssm, s_conv = _trunk(x_sample, mod[:, bp:], state_s5_re, state_s5_im, state_ssm, state_conv, p)
    return (y_p, y_s, p_re, p_im, p_ssm, p_conv, s_re, s_im, s_ssm, s_conv)
```

```python
import functools
import math

import jax
import jax.numpy as jnp
from jax import lax
from jax.experimental import pallas as pl
from jax.experimental.pallas import tpu as pltpu

F32 = jnp.float32
BF16 = jnp.bfloat16
I32 = jnp.int32

D_MODEL = 1024
DEPTH = 4
S5_GROUP = 16
S5_GROUPS = 64
S5_STATE = 64
S5_COLBLOCKS = 8
S5_BLOCK_STATE = 512
M_INNER = 2048
M_HEADDIM = 64
M_HEADS = 32
M_GROUPS = 4
M_DSTATE = 128
M_CONV = 4
M_CONV_DIM = 3072
M_CHUNK = 128
E_GROUPS = 4
E_PER_GROUP = 8
N_EXPERTS = 32
E_FF = 512
EPS = 1e-6

LANES = 128
SUBLANES = 8
VMEM_LIMIT = 56 * 1024 * 1024
COLT = 1024
MAIN_COLS = 8192
COL_U, COL_Z, COL_XBC, COL_GA, COL_GB = 0, 1024, 3072, 6144, 7168
EXPERT_TILE = 256
DMA_UNROLL = 8


def _cparams(n_axes, **kw):
    return pltpu.CompilerParams(dimension_semantics=("arbitrary",) * n_axes,
                                vmem_limit_bytes=VMEM_LIMIT, **kw)


def _dot(a, b):
    return jnp.dot(a, b, preferred_element_type=F32)


def _silu(x):
    return x / (1.0 + jnp.exp(-x))


def _sigmoid(x):
    return 1.0 / (1.0 + jnp.exp(-x))


def _gelu_tanh(x):
    return 0.5 * x * (1.0 + jnp.tanh(math.sqrt(2.0 / math.pi) * (x + 0.044715 * (x * x * x))))


def _softplus(x):
    return jnp.maximum(x, 0.0) + jnp.log(1.0 + jnp.exp(-jnp.abs(x)))


def _split3(v):
    hi = v.astype(BF16)
    r = v - hi.astype(F32)
    mid = r.astype(BF16)
    lo = (r - mid.astype(F32)).astype(BF16)
    return hi, mid, lo


def _mod_kernel(c_ref, w_ref, b_ref, o_ref):
    c = c_ref[...]
    o_ref[...] = _dot(_silu(c).astype(BF16), w_ref[...].astype(BF16)) + b_ref[...]


def _modulation(c_all, w_mod, b_mod):
    n = c_all.shape[0]
    return pl.pallas_call(
        _mod_kernel,
        grid=(DEPTH, 6),
        in_specs=[pl.BlockSpec((n, D_MODEL), lambda l, j: (0, 0)),
                  pl.BlockSpec((None, D_MODEL, D_MODEL), lambda l, j: (l, 0, j)),
                  pl.BlockSpec((None, 1, D_MODEL), lambda l, j: (l, 0, j))],
        out_specs=pl.BlockSpec((None, n, D_MODEL), lambda l, j: (l, 0, j)),
        out_shape=jax.ShapeDtypeStruct((DEPTH, n, 6 * D_MODEL), F32),
        compiler_params=_cparams(2),
        name="modulation",
    )(c_all, w_mod, b_mod.reshape(DEPTH, 1, 6 * D_MODEL))


def _s5_disc_kernel(are_ref, aim_ref, ldt_ref, bre_ref, bim_ref, abr_ref, abi_ref, bbr_ref, bbi_ref):
    a_re, a_im = are_ref[...], aim_ref[...]
    dt = jnp.exp(ldt_ref[...])
    mag = jnp.exp(a_re * dt)
    ab_re = mag * jnp.cos(a_im * dt)
    ab_im = mag * jnp.sin(a_im * dt)
    abr_ref[...] = ab_re
    abi_ref[...] = ab_im
    nr, ni = ab_re - 1.0, ab_im
    den = a_re * a_re + a_im * a_im
    co_re = (nr * a_re + ni * a_im) / den
    co_im = (ni * a_re - nr * a_im) / den
    b_re, b_im = bre_ref[...], bim_ref[...]
    cr, ci = co_re[:, None, :], co_im[:, None, :]
    bbr_ref[...] = cr * b_re - ci * b_im
    bbi_ref[...] = cr * b_im + ci * b_re


def _s5_discretise(a_re, a_im, log_dt, b_re, b_im):
    g, n, p = S5_GROUPS, S5_STATE, S5_GROUP
    bt_re = jnp.swapaxes(b_re, -1, -2)
    bt_im = jnp.swapaxes(b_im, -1, -2)
    s2 = pl.BlockSpec((None, g, n), lambda l: (l, 0, 0))
    s3 = pl.BlockSpec((None, g, p, n), lambda l: (l, 0, 0, 0))
    return pl.pallas_call(
        _s5_disc_kernel,
        grid=(DEPTH,),
        in_specs=[s2, s2, pl.BlockSpec((None, g, 1), lambda l: (l, 0, 0)), s3, s3],
        out_specs=[s2, s2, s3, s3],
        out_shape=[jax.ShapeDtypeStruct((DEPTH, g, n), F32)] * 2 + [jax.ShapeDtypeStruct((DEPTH, g, p, n), F32)] * 2,
        compiler_params=_cparams(1),
        name="s5_discretise",
    )(a_re, a_im, log_dt.reshape(DEPTH, g, 1), bt_re, bt_im)


def _s5_weights(ab_re, ab_im, bb_re, bb_im, c_re, c_im):
    cb, gl = S5_COLBLOCKS, S5_GROUPS // S5_COLBLOCKS
    eye = jnp.eye(gl, dtype=F32)

    def bdiag_in(bt):
        bt = bt.reshape(DEPTH, cb, gl, S5_GROUP, S5_STATE)
        return jnp.einsum('ljgpn,gh->ljgphn', bt, eye).reshape(DEPTH, cb, gl * S5_GROUP, gl * S5_STATE)

    def bdiag_out(c):
        c = c.reshape(DEPTH, cb, gl, S5_GROUP, S5_STATE)
        return jnp.einsum('ljgpn,gh->ljgnhp', c, eye).reshape(DEPTH, cb, gl * S5_STATE, gl * S5_GROUP)

    wb = jnp.concatenate([bdiag_in(bb_re), bdiag_in(bb_im)], axis=-1).astype(BF16)
    wc = jnp.concatenate([bdiag_out(c_re), -bdiag_out(c_im)], axis=-2).astype(BF16)
    ab = jnp.concatenate([ab_re.reshape(DEPTH, cb, 1, S5_BLOCK_STATE),
                          ab_im.reshape(DEPTH, cb, 1, S5_BLOCK_STATE)], axis=-1)
    return wb, wc, ab


N_HEAD_TILES = COL_GA // COLT


def _in_proj_kernel(x_ref, nw_ref, sh_ref, sc_ref, w_ref, wt_ref, wdt_ref, main_ref, dt_ref, h_scr):
    j = pl.program_id(1)

    @pl.when(j == 0)
    def _():
        x = x_ref[...]
        xn = x * lax.rsqrt(jnp.mean(x * x, axis=-1, keepdims=True) + EPS) * nw_ref[...]
        hb = (xn * (1.0 + sc_ref[...]) + sh_ref[...]).astype(BF16)
        h_scr[...] = hb
        dt_ref[...] = _dot(hb, wdt_ref[...])

    @pl.when(j < N_HEAD_TILES)
    def _():
        main_ref[...] = _dot(h_scr[...], w_ref[...].astype(BF16))

    @pl.when(j >= N_HEAD_TILES)
    def _():
        main_ref[...] = _dot(h_scr[...], wt_ref[...])


def _mod_spec(k, rm, tiles_per_block):
    return pl.BlockSpec((None, None, rm, D_MODEL), lambda i, *_: (k, i // tiles_per_block, 0, 0))


def _in_proj(x, norm_w, mod4, tpb, w_in, l, w_tail, w_dt, tm):
    t = x.shape[0]
    rm = mod4.shape[2]
    return pl.pallas_call(
        _in_proj_kernel,
        grid=(t // tm, MAIN_COLS // COLT),
        in_specs=[pl.BlockSpec((tm, D_MODEL), lambda i, j: (i, 0)),
                  pl.BlockSpec((1, D_MODEL), lambda i, j: (0, 0)),
                  _mod_spec(0, rm, tpb), _mod_spec(1, rm, tpb),
                  pl.BlockSpec((None, D_MODEL, COLT), lambda i, j: (l, 0, jnp.minimum(j, N_HEAD_TILES - 1))),
                  pl.BlockSpec((D_MODEL, COLT), lambda i, j: (0, jnp.maximum(j - N_HEAD_TILES, 0))),
                  pl.BlockSpec((D_MODEL, LANES), lambda i, j: (0, 0))],
        out_specs=[pl.BlockSpec((tm, COLT), lambda i, j: (i, j)),
                   pl.BlockSpec((tm, LANES), lambda i, j: (i, 0))],
        out_shape=[jax.ShapeDtypeStruct((t, MAIN_COLS), F32), jax.ShapeDtypeStruct((t, LANES), F32)],
        scratch_shapes=[pltpu.VMEM((tm, D_MODEL), BF16)],
        compiler_params=_cparams(2),
        name="in_proj",
    )(x, norm_w, mod4, mod4, w_in, w_tail, w_dt)


def _s5_kernel(u_ref, wb_ref, wc_ref, ab_ref, d_ref, h0_ref, y_ref, hl_ref, bu_scr, hs_scr, st_scr, ub_scr, ut_scr,
               *, nb, tq, time_major):
    ns = S5_BLOCK_STATE
    rows = nb * tq
    tt = pl.program_id(1)

    @pl.when(tt == 0)
    def _():
        st_scr[...] = h0_ref[...]

    if time_major:
        u_tm = u_ref[...]
    else:
        ub_scr[...] = u_ref[...].reshape(rows, LANES)
        for s in range(tq):
            ut_scr[s * nb:(s + 1) * nb, :] = ub_scr[pl.ds(s, nb, stride=tq), :]
        u_tm = ut_scr[...]
    bu_scr[...] = _dot(u_tm.astype(BF16), wb_ref[...])
    ar = jnp.broadcast_to(ab_ref[:, :ns], (nb, ns))
    ai = jnp.broadcast_to(ab_ref[:, ns:], (nb, ns))

    def step(s, carry):
        hr, hi = carry
        r0 = pl.multiple_of(s * nb, nb)
        nhr = ar * hr - ai * hi + bu_scr[pl.ds(r0, nb), :ns]
        nhi = ar * hi + ai * hr + bu_scr[pl.ds(r0, nb), ns:]
        hs_scr[pl.ds(r0, nb), :ns] = nhr
        hs_scr[pl.ds(r0, nb), ns:] = nhi
        return nhr, nhi

    hr, hi = lax.fori_loop(0, tq, step, (st_scr[:, :ns], st_scr[:, ns:]), unroll=min(tq, 8))
    st_scr[:, :ns] = hr
    st_scr[:, ns:] = hi
    y_tm = _dot(hs_scr[...].astype(BF16), wc_ref[...])
    if time_major:
        y_ref[...] = _gelu_tanh(y_tm + d_ref[...] * u_tm)
    else:
        ut_scr[...] = y_tm
        for b in range(nb):
            y_b = ut_scr[pl.ds(b, tq, stride=nb), :] + d_ref[...] * ub_scr[b * tq:(b + 1) * tq, :]
            y_ref[b] = _gelu_tanh(y_b)

    @pl.when(tt == pl.num_programs(1) - 1)
    def _():
        hl_ref[...] = st_scr[...]


def _s5_branch(main, wb, wc, ab, d_skip, h0, nb, seq, time_major):
    t = main.shape[0]
    cb = S5_COLBLOCKS
    ucol = COL_U // LANES
    tq = min(seq, 512 // nb)
    rows = tq * nb
    if time_major:
        u_in, u_spec = main, pl.BlockSpec((rows, LANES), lambda j, i: (i, ucol + j))
        y_shape, y_spec = (t, D_MODEL), pl.BlockSpec((rows, LANES), lambda j, i: (i, j))
    else:
        u_in, u_spec = main.reshape(nb, seq, MAIN_COLS), pl.BlockSpec((nb, tq, LANES), lambda j, i: (0, i, ucol + j))
        y_shape, y_spec = (nb, seq, D_MODEL), pl.BlockSpec((nb, tq, LANES), lambda j, i: (0, i, j))
    y, hl = pl.pallas_call(
        functools.partial(_s5_kernel, nb=nb, tq=tq, time_major=time_major),
        grid=(cb, seq // tq),
        in_specs=[u_spec,
                  pl.BlockSpec((None, LANES, 2 * S5_BLOCK_STATE), lambda j, i: (j, 0, 0)),
                  pl.BlockSpec((None, 2 * S5_BLOCK_STATE, LANES), lambda j, i: (j, 0, 0)),
                  pl.BlockSpec((None, 1, 2 * S5_BLOCK_STATE), lambda j, i: (j, 0, 0)),
                  pl.BlockSpec((1, LANES), lambda j, i: (0, j)),
                  pl.BlockSpec((None, nb, 2 * S5_BLOCK_STATE), lambda j, i: (j, 0, 0))],
        out_specs=[y_spec, pl.BlockSpec((None, nb, 2 * S5_BLOCK_STATE), lambda j, i: (j, 0, 0))],
        out_shape=[jax.ShapeDtypeStruct(y_shape, F32),
                   jax.ShapeDtypeStruct((cb, nb, 2 * S5_BLOCK_STATE), F32)],
        scratch_shapes=[pltpu.VMEM((rows, 2 * S5_BLOCK_STATE), F32),
                        pltpu.VMEM((rows, 2 * S5_BLOCK_STATE), F32),
                        pltpu.VMEM((nb, 2 * S5_BLOCK_STATE), F32),
                        pltpu.VMEM((rows, LANES), F32),
                        pltpu.VMEM((rows, LANES), F32)],
        compiler_params=_cparams(2),
        name="s5_branch",
    )(u_in, wb, wc, ab, d_skip, h0)
    return y.reshape(t, D_MODEL), hl


def _ssd_kernel(x0_ref, x1_ref, bc_ref, z0_ref, z1_ref, dt_ref, conv0_ref, h0_ref, cw_ref, cb_ref, dtb_ref,
                alog_ref, dsk_ref, nw_ref, tril_ref, y_ref, hout_ref, convout_ref,
                xp_scr, xbc_scr, dt_scr, y_scr, xw_scr, st_scr, *, lr):
    q = M_CHUNK
    c = pl.program_id(1)
    last = pl.num_programs(1) - 1
    hp = M_HEADDIM
    gw = M_INNER // M_GROUPS

    @pl.when(c == 0)
    def _():
        st_scr[...] = h0_ref[...]
        xp_scr[0:SUBLANES, :] = conv0_ref[...]
        if lr < q:
            xp_scr[SUBLANES:, :] = jnp.zeros((q, M_CONV_DIM), F32)
            dt_scr[...] = jnp.zeros((q, LANES), F32)

    xp_scr[SUBLANES:SUBLANES + lr, 0:COLT] = x0_ref[...]
    xp_scr[SUBLANES:SUBLANES + lr, COLT:2 * COLT] = x1_ref[...]
    xp_scr[SUBLANES:SUBLANES + lr, 2 * COLT:3 * COLT] = bc_ref[...]
    acc = cb_ref[...] + cw_ref[0:1, :] * xp_scr[5:5 + q, :]
    for k in range(1, M_CONV):
        acc = acc + cw_ref[k:k + 1, :] * xp_scr[5 + k:5 + k + q, :]
    xbc_scr[...] = _silu(acc)
    xbc = xbc_scr

    win = ((lr + 5) // SUBLANES) * SUBLANES

    @pl.when(c == last)
    def _():
        convout_ref[...] = xp_scr[win:win + SUBLANES, :]

    if lr == q:
        xp_scr[0:SUBLANES, :] = xp_scr[q:q + SUBLANES, :]

    dt_scr[0:lr, :] = _softplus(dt_ref[...] + dtb_ref[...])
    dt = dt_scr[...]
    da = dt * (-jnp.exp(alog_ref[...]))
    tril = tril_ref[...]
    p3 = _split3(da)
    acum = _dot(tril, p3[0]) + _dot(tril, p3[1]) + _dot(tril, p3[2])
    acum_t = acum.T

    row_i = lax.broadcasted_iota(I32, (q, q), 0)
    col_j = lax.broadcasted_iota(I32, (q, q), 1)
    causal = row_i >= col_j
    low_half = lax.broadcasted_iota(I32, (q, LANES), 1) < hp

    for g in range(M_GROUPS):
        bm = xbc[:, M_INNER + g * M_DSTATE:M_INNER + (g + 1) * M_DSTATE].astype(BF16)
        cm = xbc[:, M_INNER + M_GROUPS * M_DSTATE + g * M_DSTATE:
                 M_INNER + M_GROUPS * M_DSTATE + (g + 1) * M_DSTATE].astype(BF16)
        cbm = lax.dot_general(cm, bm, (((1,), (1,)), ((), ())), preferred_element_type=F32)
        st_g = st_scr[g * gw:(g + 1) * gw, :]
        y_off = lax.dot_general(cm, st_g.astype(BF16), (((1,), (1,)), ((), ())),
                                preferred_element_type=F32)
        for kk in range(gw // LANES):
            k = g * (gw // LANES) + kk
            cols = slice(k * LANES, (k + 1) * LANES)
            h_a, h_b = 2 * k, 2 * k + 1
            col_a = jnp.broadcast_to(acum[:, h_a:h_a + 1], (q, LANES))
            col_b = jnp.broadcast_to(acum[:, h_b:h_b + 1], (q, LANES))
            dt_p = jnp.where(low_half, jnp.broadcast_to(dt[:, h_a:h_a + 1], (q, LANES)),
                             jnp.broadcast_to(dt[:, h_b:h_b + 1], (q, LANES)))
            ac_p = jnp.where(low_half, col_a, col_b)
            end_p = jnp.broadcast_to(ac_p[q - 1:q, :], (q, LANES))
            xs_p = xbc[:, cols]
            xdt = xs_p * dt_p
            xdt_b = xdt.astype(BF16)
            l_a = jnp.where(causal, jnp.exp(jnp.minimum(col_a - acum_t[h_a:h_a + 1, :], 0.0)), 0.0)
            l_b = jnp.where(causal, jnp.exp(jnp.minimum(col_b - acum_t[h_b:h_b + 1, :], 0.0)), 0.0)
            r_a = _dot((cbm * l_a).astype(BF16), xdt_b)
            r_b = _dot((cbm * l_b).astype(BF16), xdt_b)
            y_scr[:, cols] = (jnp.where(low_half, r_a, r_b) + y_off[:, kk * LANES:(kk + 1) * LANES] * jnp.exp(ac_p)
                              + dsk_ref[:, cols] * xs_p)
            xw_scr[:, cols] = xdt * jnp.exp(end_p - ac_p)
            dec = jnp.exp(ac_p[q - 1:q, :])
            dec_a = jnp.broadcast_to(dec[:, 0:1], (hp, M_DSTATE))
            dec_b = jnp.broadcast_to(dec[:, hp:hp + 1], (hp, M_DSTATE))
            st_scr[h_a * hp:(h_a + 1) * hp, :] = st_scr[h_a * hp:(h_a + 1) * hp, :] * dec_a
            st_scr[h_b * hp:(h_b + 1) * hp, :] = st_scr[h_b * hp:(h_b + 1) * hp, :] * dec_b
        xw_g = xw_scr[:, g * gw:(g + 1) * gw].astype(BF16)
        st_scr[g * gw:(g + 1) * gw, :] += lax.dot_general(xw_g, bm, (((0,), (0,)), ((), ())),
                                                          preferred_element_type=F32)

    for g in range(M_GROUPS):
        z_ref = z0_ref if g < M_GROUPS // 2 else z1_ref
        zc = (g % (M_GROUPS // 2)) * gw
        v = y_scr[0:lr, g * gw:(g + 1) * gw] * _silu(z_ref[:, zc:zc + gw])
        y_ref[:, g * gw:(g + 1) * gw] = (v * lax.rsqrt(jnp.mean(v * v, axis=-1, keepdims=True) + EPS)
                                         * nw_ref[:, g * gw:(g + 1) * gw])

    @pl.when(c == last)
    def _():
        hout_ref[...] = st_scr[...]


def _ssd_branch(main, dt_raw, conv0, h0, cw, cb, dtb, alog, dsk, nw, tril, nb, seq, time_major):
    q = M_CHUNK
    lr = min(q, seq)
    nc = seq // lr
    nct = MAIN_COLS // COLT
    if time_major:
        main_v, dt_v = main.reshape(seq, nb * MAIN_COLS), dt_raw.reshape(seq, nb * LANES)
        rowcol = lambda b, c, k: (c, nct * b + k)
        dt_map = lambda b, c: (c, b)
        y_shape, y_map = (seq, nb * M_INNER), (lambda b, c: (c, b))
    else:
        main_v, dt_v = main, dt_raw
        rowcol = lambda b, c, k: (b * nc + c, k)
        dt_map = lambda b, c: (b * nc + c, 0)
        y_shape, y_map = (seq * nb, M_INNER), (lambda b, c: (b * nc + c, 0))
    tile = lambda k: pl.BlockSpec((lr, COLT), lambda b, c: rowcol(b, c, k))
    const = lambda shape: pl.BlockSpec(shape, lambda b, c: (0,) * len(shape))
    kx, kz = COL_XBC // COLT, COL_Z // COLT
    y, hout, convout = pl.pallas_call(
        functools.partial(_ssd_kernel, lr=lr),
        grid=(nb, nc),
        in_specs=[tile(kx), tile(kx + 1), tile(kx + 2), tile(kz), tile(kz + 1),
                  pl.BlockSpec((lr, LANES), dt_map),
                  pl.BlockSpec((None, SUBLANES, M_CONV_DIM), lambda b, c: (b, 0, 0)),
                  pl.BlockSpec((None, M_INNER, M_DSTATE), lambda b, c: (b, 0, 0)),
                  const((SUBLANES, M_CONV_DIM)), const((1, M_CONV_DIM)), const((1, LANES)), const((1, LANES)),
                  const((1, M_INNER)), const((1, M_INNER)), const((q, q))],
        out_specs=[pl.BlockSpec((lr, M_INNER), y_map),
                   pl.BlockSpec((None, M_INNER, M_DSTATE), lambda b, c: (b, 0, 0)),
                   pl.BlockSpec((None, SUBLANES, M_CONV_DIM), lambda b, c: (b, 0, 0))],
        out_shape=[jax.ShapeDtypeStruct(y_shape, F32),
                   jax.ShapeDtypeStruct((nb, M_INNER, M_DSTATE), F32),
                   jax.ShapeDtypeStruct((nb, SUBLANES, M_CONV_DIM), F32)],
        scratch_shapes=[pltpu.VMEM((q + SUBLANES, M_CONV_DIM), F32),
                        pltpu.VMEM((q, M_CONV_DIM), F32),
                        pltpu.VMEM((q, LANES), F32),
                        pltpu.VMEM((q, M_INNER), F32),
                        pltpu.VMEM((q, M_INNER), F32),
                        pltpu.VMEM((M_INNER, M_DSTATE), F32)],
        compiler_params=_cparams(2),
        name="ssd_branch",
    )(main_v, main_v, main_v, main_v, main_v, dt_v, conv0, h0, cw, cb, dtb, alog, dsk, nw, tril)
    off = lr + 5 - ((lr + 5) // SUBLANES) * SUBLANES
    return y.reshape(seq * nb, M_INNER), hout, convout[:, off:off + M_CONV - 1]


def _route(logits, cnt_scr, ltri):
    tm = logits.shape[0]
    lane = lax.broadcasted_iota(I32, (tm, LANES), 1).astype(F32)
    neg = jnp.float32(-jnp.inf)
    far = jnp.float32(LANES)
    glog = jnp.where(lane < E_GROUPS, logits, neg)
    gmax = jnp.max(glog, axis=-1, keepdims=True)
    g_idx = jnp.min(jnp.where(glog == gmax, lane, far), axis=-1, keepdims=True)
    g_p = 1.0 / jnp.sum(jnp.exp(glog - gmax), axis=-1, keepdims=True)
    e_lane = lane - E_GROUPS
    in_group = (e_lane >= g_idx * E_PER_GROUP) & (e_lane < (g_idx + 1.0) * E_PER_GROUP)
    elog = jnp.where(in_group, logits, neg)
    v0 = jnp.max(elog, axis=-1, keepdims=True)
    i0 = jnp.min(jnp.where(elog == v0, lane, far), axis=-1, keepdims=True)
    elog1 = jnp.where(lane == i0, neg, elog)
    v1 = jnp.max(elog1, axis=-1, keepdims=True)
    i1 = jnp.min(jnp.where(elog1 == v1, lane, far), axis=-1, keepdims=True)
    e1x = jnp.exp(v1 - v0)
    w0 = g_p / (1.0 + e1x)
    w1 = g_p * e1x / (1.0 + e1x)
    e0, e1 = i0 - E_GROUPS, i1 - E_GROUPS
    hit0, hit1 = lane == e0, lane == e1
    onehot = jnp.where(hit0 | hit1, 1.0, 0.0)
    before = _dot(ltri, onehot.astype(BF16)) + cnt_scr[...]
    rank0 = jnp.sum(jnp.where(hit0, before, 0.0), axis=-1, keepdims=True)
    rank1 = jnp.sum(jnp.where(hit1, before, 0.0), axis=-1, keepdims=True)
    cnt_scr[...] += jnp.sum(onehot, axis=0, keepdims=True)
    ri = jnp.where(lane == 0, e0, jnp.where(lane == 1, e1, jnp.where(lane == 2, rank0, jnp.where(lane == 3, rank1, 0.0))))
    rw = jnp.where(lane == 0, w0, jnp.where(lane == 1, w1, 0.0))
    return ri, rw


def _merge_kernel(ya_ref, yb_ref, ga_ref, gb_ref, x_ref, g1_ref, sh2_ref, sc2_ref, n2_ref,
                  wglu_ref, wpa_ref, wpb_ref, wout_ref, wr_ref, br_ref, ltri_ref,
                  xo_ref, h2_ref, ri_ref, rw_ref, cnt_ref, cnt_scr):
    i = pl.program_id(0)

    @pl.when(i == 0)
    def _():
        cnt_scr[...] = jnp.zeros_like(cnt_scr)

    ya = ya_ref[...]
    ya = ya * _sigmoid(_dot(ya.astype(BF16), wglu_ref[...]))
    pa = _dot(ya.astype(BF16), wpa_ref[...])
    pb = _dot(yb_ref[...].astype(BF16), wpb_ref[...])
    merged = _sigmoid(ga_ref[...]) * pa + _sigmoid(gb_ref[...]) * pb
    x = x_ref[...] + g1_ref[...] * _dot(merged.astype(BF16), wout_ref[...])
    xo_ref[...] = x
    xn = x * lax.rsqrt(jnp.mean(x * x, axis=-1, keepdims=True) + EPS) * n2_ref[...]
    h2 = xn * (1.0 + sc2_ref[...]) + sh2_ref[...]
    h2_ref[...] = h2
    logits = _dot(h2.astype(BF16), wr_ref[...]) + br_ref[...]
    ri, rw = _route(logits, cnt_scr, ltri_ref[...])
    ri_ref[...] = ri.T[0:SUBLANES, :].astype(I32)
    rw_ref[...] = rw

    @pl.when(i == pl.num_programs(0) - 1)
    def _():
        cnt_ref[...] = cnt_scr[...]


def _merge(ya, yb, main, x, mod4, tpb, n2, wglu, wpa, wpb, wout, wr, br, ltri, tm):
    t = x.shape[0]
    rm = mod4.shape[2]
    row = lambda w: pl.BlockSpec((tm, w), lambda i: (i, 0))
    const = lambda shape: pl.BlockSpec(shape, lambda i: (0,) * len(shape))
    return pl.pallas_call(
        _merge_kernel,
        grid=(t // tm,),
        in_specs=[row(D_MODEL), row(M_INNER),
                  pl.BlockSpec((tm, D_MODEL), lambda i: (i, COL_GA // D_MODEL)),
                  pl.BlockSpec((tm, D_MODEL), lambda i: (i, COL_GB // D_MODEL)),
                  row(D_MODEL), _mod_spec(2, rm, tpb), _mod_spec(3, rm, tpb), _mod_spec(4, rm, tpb),
                  const((1, D_MODEL)),
                  const((D_MODEL, D_MODEL)), const((D_MODEL, D_MODEL)), const((M_INNER, D_MODEL)),
                  const((D_MODEL, D_MODEL)), const((D_MODEL, LANES)), const((1, LANES)), const((tm, tm))],
        out_specs=[row(D_MODEL), row(D_MODEL), pl.BlockSpec((SUBLANES, tm), lambda i: (0, i)), row(LANES),
                   const((1, LANES))],
        out_shape=[jax.ShapeDtypeStruct((t, D_MODEL), F32), jax.ShapeDtypeStruct((t, D_MODEL), F32),
                   jax.ShapeDtypeStruct((SUBLANES, t), I32), jax.ShapeDtypeStruct((t, LANES), F32),
                   jax.ShapeDtypeStruct((1, LANES), F32)],
        scratch_shapes=[pltpu.VMEM((1, LANES), F32)],
        compiler_params=_cparams(1),
        name="merge_route",
    )(ya, yb, main, main, x, mod4, mod4, mod4, n2, wglu, wpa, wpb, wout, wr, br, ltri)


def _row_copy(src_ref, src_row, dst_ref, dst_row, sem):
    return pltpu.make_async_copy(src_ref.at[pl.ds(src_row, 1)], dst_ref.at[pl.ds(dst_row, 1)], sem)


def _dispatch_kernel(p0_ref, p1_ref, h2_ref, buf_ref, xs_ref, sem, *, tm):
    del buf_ref
    base = pl.program_id(0) * tm

    def issue(r, carry):
        _row_copy(h2_ref, r, xs_ref, p0_ref[base + r], sem).start()
        _row_copy(h2_ref, r, xs_ref, p1_ref[base + r], sem).start()
        return carry

    lax.fori_loop(0, tm, issue, 0, unroll=DMA_UNROLL)

    def drain(r, carry):
        _row_copy(h2_ref, r, xs_ref, p0_ref[base + r], sem).wait()
        _row_copy(h2_ref, r, xs_ref, p1_ref[base + r], sem).wait()
        return carry

    lax.fori_loop(0, tm, drain, 0, unroll=DMA_UNROLL)


def _dispatch(h2, pos0, pos1, buf, tm):
    t = h2.shape[0]
    return pl.pallas_call(
        functools.partial(_dispatch_kernel, tm=tm),
        grid_spec=pltpu.PrefetchScalarGridSpec(
            num_scalar_prefetch=2,
            grid=(t // tm,),
            in_specs=[pl.BlockSpec((tm, D_MODEL), lambda i, p0, p1: (i, 0)),
                      pl.BlockSpec(memory_space=pl.ANY)],
            out_specs=pl.BlockSpec(memory_space=pl.ANY),
            scratch_shapes=[pltpu.SemaphoreType.DMA(())]),
        out_shape=jax.ShapeDtypeStruct(buf.shape, F32),
        input_output_aliases={3: 0},
        compiler_params=_cparams(1, disable_bounds_checks=True),
        name="moe_dispatch",
    )(pos0, pos1, h2, buf)


def _expert_kernel(te_ref, tv_ref, xs_ref, wg_ref, wu_ref, wd_ref, o_ref):
    i = pl.program_id(0)

    @pl.when(tv_ref[i] > 0)
    def _():
        xb = xs_ref[...].astype(BF16)
        act = _silu(_dot(xb, wg_ref[...].astype(BF16))) * _dot(xb, wu_ref[...].astype(BF16))
        o_ref[...] = _dot(act.astype(BF16), wd_ref[...].astype(BF16))

    @pl.when(tv_ref[i] == 0)
    def _():
        o_ref[...] = jnp.zeros_like(o_ref)


def _experts(xs, tile_expert, tile_valid, wg, wu, wd, l):
    npad = xs.shape[0]
    te = EXPERT_TILE
    return pl.pallas_call(
        _expert_kernel,
        grid_spec=pltpu.PrefetchScalarGridSpec(
            num_scalar_prefetch=2,
            grid=(npad // te,),
            in_specs=[pl.BlockSpec((te, D_MODEL), lambda i, e, v: (i, 0)),
                      pl.BlockSpec((None, None, D_MODEL, E_FF), lambda i, e, v: (l, e[i], 0, 0)),
                      pl.BlockSpec((None, None, D_MODEL, E_FF), lambda i, e, v: (l, e[i], 0, 0)),
                      pl.BlockSpec((None, None, E_FF, D_MODEL), lambda i, e, v: (l, e[i], 0, 0))],
            out_specs=pl.BlockSpec((te, D_MODEL), lambda i, e, v: (i, 0))),
        out_shape=jax.ShapeDtypeStruct((npad, D_MODEL), F32),
        compiler_params=_cparams(1),
        name="moe_experts",
    )(tile_expert, tile_valid, xs, wg, wu, wd)


def _combine_kernel(p0_ref, p1_ref, ys_ref, x_ref, rw_ref, g2_ref, nf_ref, o_ref, b0, b1, sem, *, tm, final):
    base = pl.program_id(0) * tm

    def issue(r, carry):
        _row_copy(ys_ref, p0_ref[base + r], b0, r, sem).start()
        _row_copy(ys_ref, p1_ref[base + r], b1, r, sem).start()
        return carry

    lax.fori_loop(0, tm, issue, 0, unroll=DMA_UNROLL)

    def drain(r, carry):
        _row_copy(ys_ref, p0_ref[base + r], b0, r, sem).wait()
        _row_copy(ys_ref, p1_ref[base + r], b1, r, sem).wait()
        return carry

    lax.fori_loop(0, tm, drain, 0, unroll=DMA_UNROLL)
    rw = rw_ref[...]
    w0 = jnp.broadcast_to(rw[:, 0:1], (tm, D_MODEL))
    w1 = jnp.broadcast_to(rw[:, 1:2], (tm, D_MODEL))
    x = x_ref[...] + g2_ref[...] * (w0 * b0[...] + w1 * b1[...])
    if final:
        x = x * lax.rsqrt(jnp.mean(x * x, axis=-1, keepdims=True) + EPS) * nf_ref[...]
    o_ref[...] = x


def _combine(ys, pos0, pos1, x, rw, mod4, tpb, norm_f, tm, final):
    t = x.shape[0]
    rm = mod4.shape[2]
    row = lambda w: pl.BlockSpec((tm, w), lambda i, p0, p1: (i, 0))
    return pl.pallas_call(
        functools.partial(_combine_kernel, tm=tm, final=final),
        grid_spec=pltpu.PrefetchScalarGridSpec(
            num_scalar_prefetch=2,
            grid=(t // tm,),
            in_specs=[pl.BlockSpec(memory_space=pl.ANY), row(D_MODEL), row(LANES),
                      _mod_spec(5, rm, tpb),
                      pl.BlockSpec((1, D_MODEL), lambda i, p0, p1: (0, 0))],
            out_specs=row(D_MODEL),
            scratch_shapes=[pltpu.VMEM((tm, D_MODEL), F32), pltpu.VMEM((tm, D_MODEL), F32),
                            pltpu.SemaphoreType.DMA(())]),
        out_shape=jax.ShapeDtypeStruct((t, D_MODEL), F32),
        compiler_params=_cparams(1, disable_bounds_checks=True),
        name="moe_combine",
    )(pos0, pos1, ys, x, rw, mod4, norm_f)


def _moe_plan(routes, counts, n_rows):
    te = EXPERT_TILE
    experts = jnp.arange(N_EXPERTS, dtype=I32)
    cnts = [c[0, :N_EXPERTS].astype(I32) for c in counts]
    total = sum(cnts)
    padded = ((total + te - 1) // te) * te
    ends = jnp.cumsum(padded)
    offs = ends - padded
    pos = []
    for ri in routes:
        lookup = lambda e: jnp.sum(jnp.where(e[:, None] == experts[None, :], offs[None, :], 0), axis=1)
        pos.append(((lookup(ri[0]) + ri[2]).astype(I32), (lookup(ri[1]) + ri[3]).astype(I32)))
        offs = offs + cnts[len(pos) - 1]
    starts = jnp.arange(n_rows // te, dtype=I32) * te
    tile_expert = jnp.minimum(jnp.sum((starts[:, None] >= ends[None, :]).astype(I32), axis=1), N_EXPERTS - 1)
    tile_valid = (starts < ends[-1]).astype(I32)
    return pos, tile_expert, tile_valid


class _Group:
    def __init__(self, nb, seq, time_major):
        self.nb, self.seq, self.time_major = nb, seq, time_major
        self.t = nb * seq
        span = self.t if time_major else seq
        self.tm = min(512, span)
        self.tm_merge = min(256, span)

    def mod4(self, mod_l, tm):
        m = mod_l.reshape(self.nb, 6, D_MODEL).transpose(1, 0, 2)
        if self.time_major:
            return jnp.tile(m, (1, tm // self.nb, 1))[:, None], self.t // tm
        return m[:, :, None, :], self.seq // tm


def _mixers(g, xt, mod_l, s5_re0, s5_im0, ssm0, conv0, p, l):
    nb, seq, cb = g.nb, g.seq, S5_COLBLOCKS
    mod_in, tpb_in = g.mod4(mod_l, g.tm)
    main, dt_raw = _in_proj(xt, p['norm1'][l], mod_in, tpb_in, p['w_in'], l, p['w_tail'][l], p['w_dt'][l], g.tm)
    h0 = jnp.concatenate([s5_re0.reshape(nb, cb, S5_BLOCK_STATE), s5_im0.reshape(nb, cb, S5_BLOCK_STATE)],
                         axis=-1).transpose(1, 0, 2)
    ya, hl = _s5_branch(main, p['wb'][l], p['wc'][l], p['ab'][l], p['s5_d'][l], h0, nb, seq, g.time_major)
    hl = hl.transpose(1, 0, 2)
    new_re = hl[..., :S5_BLOCK_STATE].reshape(nb, S5_GROUPS, S5_STATE)
    new_im = hl[..., S5_BLOCK_STATE:].reshape(nb, S5_GROUPS, S5_STATE)
    conv_in = jnp.pad(conv0, ((0, 0), (SUBLANES - (M_CONV - 1), 0), (0, 0)))
    yb, h_ssm, cbuf = _ssd_branch(main, dt_raw, conv_in, ssm0.reshape(nb, M_INNER, M_DSTATE),
                                  p['conv_w'][l], p['conv_b'][l], p['dt_bias'][l], p['a_log'][l],
                                  p['m_d'][l], p['m_norm'][l], p['tril'], nb, seq, g.time_major)
    mod_mg, tpb_mg = g.mod4(mod_l, g.tm_merge)
    xt, h2, ri, rw, counts = _merge(ya, yb, main, xt, mod_mg, tpb_mg, p['norm2'][l], p['w_glu'][l],
                                    p['w_pa'][l], p['w_pb'][l], p['w_out'][l], p['w_r'][l], p['b_r'][l],
                                    p['ltri'][:g.tm_merge, :g.tm_merge], g.tm_merge)
    states = (new_re, new_im, h_ssm.reshape(nb, M_HEADS, M_HEADDIM, M_DSTATE), cbuf)
    return xt, h2, ri, rw, counts, mod_in, tpb_in, states


def _prepare(norm1, norm2, w_in, s5_a_re, s5_a_im, s5_log_dt, s5_b_re, s5_b_im, s5_c_re, s5_c_im, s5_d, w_glu,
             conv_w, conv_b, dt_bias, a_log, m_d, m_norm, w_pa, w_pb, w_out, router_g, router_g_b, router_e,
             router_e_b, w_gate, w_up, w_down, norm_f):
    c_dt = COL_GA
    w_dt = jnp.pad(w_in[..., c_dt:c_dt + M_HEADS], ((0, 0), (0, 0), (0, LANES - M_HEADS))).astype(BF16)
    w_tail = w_in[..., c_dt + M_HEADS:].astype(BF16)
    ab_re, ab_im, bb_re, bb_im = _s5_discretise(s5_a_re, s5_a_im, s5_log_dt, s5_b_re, s5_b_im)
    wb, wc, ab = _s5_weights(ab_re, ab_im, bb_re, bb_im, s5_c_re, s5_c_im)
    pad_heads = lambda v: jnp.pad(v, ((0, 0), (0, LANES - M_HEADS))).reshape(DEPTH, 1, LANES)
    w_r = jnp.pad(jnp.concatenate([router_g, router_e], axis=-1),
                  ((0, 0), (0, 0), (0, LANES - E_GROUPS - N_EXPERTS))).astype(BF16)
    b_r = jnp.pad(jnp.concatenate([router_g_b, router_e_b], axis=-1),
                  ((0, 0), (0, LANES - E_GROUPS - N_EXPERTS))).reshape(DEPTH, 1, LANES)
    idx = jnp.arange(512)
    return dict(
        norm1=norm1.reshape(DEPTH, 1, D_MODEL), norm2=norm2.reshape(DEPTH, 1, D_MODEL),
        norm_f=norm_f.reshape(1, D_MODEL), w_in=w_in, w_tail=w_tail, w_dt=w_dt, wb=wb, wc=wc, ab=ab,
        s5_d=s5_d.reshape(DEPTH, 1, D_MODEL), w_glu=w_glu.astype(BF16),
        conv_w=jnp.pad(conv_w, ((0, 0), (0, SUBLANES - M_CONV), (0, 0))),
        conv_b=conv_b.reshape(DEPTH, 1, M_CONV_DIM), dt_bias=pad_heads(dt_bias), a_log=pad_heads(a_log),
        m_d=jnp.repeat(m_d, M_HEADDIM, axis=-1).reshape(DEPTH, 1, M_INNER), m_norm=m_norm.reshape(DEPTH, 1, M_INNER),
        w_pa=w_pa.astype(BF16), w_pb=w_pb.astype(BF16), w_out=w_out.astype(BF16), w_r=w_r, b_r=b_r,
        w_gate=w_gate.reshape(DEPTH, N_EXPERTS, D_MODEL, E_FF),
        w_up=w_up.reshape(DEPTH, N_EXPERTS, D_MODEL, E_FF),
        w_down=w_down.reshape(DEPTH, N_EXPERTS, E_FF, D_MODEL),
        tril=(idx[:M_CHUNK, None] >= idx[None, :M_CHUNK]).astype(BF16),
        ltri=(idx[:, None] > idx[None, :]).astype(BF16),
    )


def _forward(xs, mods, states, groups, p):
    n_rows = 2 * sum(g.t for g in groups) + N_EXPERTS * EXPERT_TILE
    new_states = [[] for _ in groups]
    for l in range(DEPTH):
        mixed = [_mixers(g, x, m[l], *[s[l] for s in st], p, l) for g, x, m, st in zip(groups, xs, mods, states)]
        pos, tile_expert, tile_valid = _moe_plan([m[2] for m in mixed], [m[4] for m in mixed], n_rows)
        buf = jnp.zeros((n_rows, D_MODEL), F32)
        for g, m, (p0, p1) in zip(groups, mixed, pos):
            buf = _dispatch(m[1], p0, p1, buf, g.tm)
        ys = _experts(buf, tile_expert, tile_valid, p['w_gate'], p['w_up'], p['w_down'], l)
        xs = [_combine(ys, p0, p1, m[0], m[3], m[5], m[6], p['norm_f'], g.tm, l == DEPTH - 1)
              for g, m, (p0, p1) in zip(groups, mixed, pos)]
        for k, m in enumerate(mixed):
            new_states[k].append(m[7])
    stacked = [tuple(jnp.stack([layer[i] for layer in ns]) for i in range(4)) for ns in new_states]
    return xs, stacked


def kernel(x_prompt, x_sample, state_s5_re, state_s5_im, state_ssm, state_conv, c_prompt, c_sample, norm1, norm2, w_mod, b_mod, w_in, s5_a_re, s5_a_im, s5_log_dt, s5_b_re, s5_b_im, s5_c_re, s5_c_im, s5_d, w_glu, conv_w, conv_b, dt_bias, a_log, m_d, m_norm, w_pa, w_pb, w_out, router_g, router_g_b, router_e, router_e_b, w_gate, w_up, w_down, norm_f):
    p = _prepare(norm1, norm2, w_in, s5_a_re, s5_a_im, s5_log_dt, s5_b_re, s5_b_im, s5_c_re, s5_c_im, s5_d, w_glu,
                 conv_w, conv_b, dt_bias, a_log, m_d, m_norm, w_pa, w_pb, w_out, router_g, router_g_b, router_e,
                 router_e_b, w_gate, w_up, w_down, norm_f)
    bp, lp, _ = x_prompt.shape
    bs, ls, _ = x_sample.shape
    groups = [_Group(bp, lp, time_major=False), _Group(bs, ls, time_major=True)]
    mod = _modulation(jnp.concatenate([c_prompt, c_sample], axis=0), w_mod, b_mod)
    zero = lambda s: jnp.zeros((DEPTH, bp) + s.shape[2:], s.dtype)
    states = [(zero(state_s5_re), zero(state_s5_im), zero(state_ssm), zero(state_conv)),
              (state_s5_re, state_s5_im, state_ssm, state_conv)]
    xs = [x_prompt.reshape(bp * lp, D_MODEL), jnp.swapaxes(x_sample, 0, 1).reshape(bs * ls, D_MODEL)]
    (y_p, y_s), (st_p, st_s) = _forward(xs, [mod[:, :bp], mod[:, bp:]], states, groups, p)
    y_p = y_p.reshape(bp, lp, D_MODEL)
    y_s = jnp.swapaxes(y_s.reshape(ls, bs, D_MODEL), 0, 1)
    return (y_p, y_s) + st_p + st_s
```

```python
import functools
import math

import jax
import jax.numpy as jnp
from jax import lax
from jax.experimental import pallas as pl
from jax.experimental.pallas import tpu as pltpu

F32 = jnp.float32
BF16 = jnp.bfloat16
I32 = jnp.int32

D_MODEL = 1024
DEPTH = 4
S5_GROUP = 16
S5_GROUPS = 64
S5_STATE = 64
S5_COLBLOCKS = 8
S5_BLOCK_STATE = 512
M_INNER = 2048
M_HEADDIM = 64
M_HEADS = 32
M_GROUPS = 4
M_DSTATE = 128
M_CONV = 4
M_CONV_DIM = 3072
M_CHUNK = 128
E_GROUPS = 4
E_PER_GROUP = 8
N_EXPERTS = 32
E_FF = 512
EPS = 1e-6

LANES = 128
SUBLANES = 8
VMEM_LIMIT = 56 * 1024 * 1024
COLT = 1024
MAIN_COLS = 8192
COL_U, COL_Z, COL_XBC, COL_GA, COL_GB = 0, 1024, 3072, 6144, 7168
EXPERT_TILE = 256
DMA_UNROLL = 8


def _cparams(n_axes, **kw):
    return pltpu.CompilerParams(dimension_semantics=("arbitrary",) * n_axes,
                                vmem_limit_bytes=VMEM_LIMIT, **kw)


def _dot(a, b):
    return jnp.dot(a, b, preferred_element_type=F32)


def _sigmoid(x):
    return 0.5 + 0.5 * jnp.tanh(0.5 * x)


def _silu(x):
    return x * _sigmoid(x)


def _gelu_tanh(x):
    return 0.5 * x * (1.0 + jnp.tanh(math.sqrt(2.0 / math.pi) * (x + 0.044715 * (x * x * x))))


def _softplus(x):
    return jnp.maximum(x, 0.0) + jnp.log(1.0 + jnp.exp(-jnp.abs(x)))


def _split3(v):
    hi = v.astype(BF16)
    r = v - hi.astype(F32)
    mid = r.astype(BF16)
    lo = (r - mid.astype(F32)).astype(BF16)
    return hi, mid, lo


def _mod_kernel(c_ref, w_ref, b_ref, o_ref):
    c = c_ref[...]
    o_ref[...] = _dot(_silu(c).astype(BF16), w_ref[...].astype(BF16)) + b_ref[...]


def _modulation(c_all, w_mod, b_mod):
    n = c_all.shape[0]
    return pl.pallas_call(
        _mod_kernel,
        grid=(DEPTH, 6),
        in_specs=[pl.BlockSpec((n, D_MODEL), lambda l, j: (0, 0)),
                  pl.BlockSpec((None, D_MODEL, D_MODEL), lambda l, j: (l, 0, j)),
                  pl.BlockSpec((None, 1, D_MODEL), lambda l, j: (l, 0, j))],
        out_specs=pl.BlockSpec((None, n, D_MODEL), lambda l, j: (l, 0, j)),
        out_shape=jax.ShapeDtypeStruct((DEPTH, n, 6 * D_MODEL), F32),
        compiler_params=_cparams(2),
        name="modulation",
    )(c_all, w_mod, b_mod.reshape(DEPTH, 1, 6 * D_MODEL))


def _s5_disc_kernel(are_ref, aim_ref, ldt_ref, bre_ref, bim_ref, abr_ref, abi_ref, bbr_ref, bbi_ref):
    a_re, a_im = are_ref[...], aim_ref[...]
    dt = jnp.exp(ldt_ref[...])
    mag = jnp.exp(a_re * dt)
    ab_re = mag * jnp.cos(a_im * dt)
    ab_im = mag * jnp.sin(a_im * dt)
    abr_ref[...] = ab_re
    abi_ref[...] = ab_im
    nr, ni = ab_re - 1.0, ab_im
    den = a_re * a_re + a_im * a_im
    co_re = (nr * a_re + ni * a_im) / den
    co_im = (ni * a_re - nr * a_im) / den
    b_re, b_im = bre_ref[...], bim_ref[...]
    cr, ci = co_re[:, None, :], co_im[:, None, :]
    bbr_ref[...] = cr * b_re - ci * b_im
    bbi_ref[...] = cr * b_im + ci * b_re


def _s5_discretise(a_re, a_im, log_dt, b_re, b_im):
    g, n, p = S5_GROUPS, S5_STATE, S5_GROUP
    bt_re = jnp.swapaxes(b_re, -1, -2)
    bt_im = jnp.swapaxes(b_im, -1, -2)
    s2 = pl.BlockSpec((None, g, n), lambda l: (l, 0, 0))
    s3 = pl.BlockSpec((None, g, p, n), lambda l: (l, 0, 0, 0))
    return pl.pallas_call(
        _s5_disc_kernel,
        grid=(DEPTH,),
        in_specs=[s2, s2, pl.BlockSpec((None, g, 1), lambda l: (l, 0, 0)), s3, s3],
        out_specs=[s2, s2, s3, s3],
        out_shape=[jax.ShapeDtypeStruct((DEPTH, g, n), F32)] * 2 + [jax.ShapeDtypeStruct((DEPTH, g, p, n), F32)] * 2,
        compiler_params=_cparams(1),
        name="s5_discretise",
    )(a_re, a_im, log_dt.reshape(DEPTH, g, 1), bt_re, bt_im)


def _s5_weights(ab_re, ab_im, bb_re, bb_im, c_re, c_im):
    cb, gl = S5_COLBLOCKS, S5_GROUPS // S5_COLBLOCKS
    eye = jnp.eye(gl, dtype=F32)

    def bdiag_in(bt):
        bt = bt.reshape(DEPTH, cb, gl, S5_GROUP, S5_STATE)
        return jnp.einsum('ljgpn,gh->ljgphn', bt, eye).reshape(DEPTH, cb, gl * S5_GROUP, gl * S5_STATE)

    def bdiag_out(c):
        c = c.reshape(DEPTH, cb, gl, S5_GROUP, S5_STATE)
        return jnp.einsum('ljgpn,gh->ljgnhp', c, eye).reshape(DEPTH, cb, gl * S5_STATE, gl * S5_GROUP)

    wb = jnp.concatenate([bdiag_in(bb_re), bdiag_in(bb_im)], axis=-1).astype(BF16)
    wc = jnp.concatenate([bdiag_out(c_re), -bdiag_out(c_im)], axis=-2).astype(BF16)
    ab = jnp.concatenate([ab_re.reshape(DEPTH, cb, 1, S5_BLOCK_STATE),
                          ab_im.reshape(DEPTH, cb, 1, S5_BLOCK_STATE)], axis=-1)
    return wb, wc, ab


N_HEAD_TILES = COL_GA // COLT


def _in_proj_kernel(x_ref, nw_ref, sh_ref, sc_ref, w_ref, wt_ref, wdt_ref, main_ref, dt_ref, h_scr):
    j = pl.program_id(1)

    @pl.when(j == 0)
    def _():
        x = x_ref[...]
        xn = x * lax.rsqrt(jnp.mean(x * x, axis=-1, keepdims=True) + EPS) * nw_ref[...]
        hb = (xn * (1.0 + sc_ref[...]) + sh_ref[...]).astype(BF16)
        h_scr[...] = hb
        dt_ref[...] = _dot(hb, wdt_ref[...])

    @pl.when(j < N_HEAD_TILES)
    def _():
        main_ref[...] = _dot(h_scr[...], w_ref[...])

    @pl.when(j >= N_HEAD_TILES)
    def _():
        main_ref[...] = _dot(h_scr[...], wt_ref[...])


def _mod_spec(k, rm, tiles_per_block):
    return pl.BlockSpec((None, None, rm, D_MODEL), lambda i, *_: (k, i // tiles_per_block, 0, 0))


def _in_proj(x, norm_w, mod4, tpb, w_in, l, w_tail, w_dt, tm):
    t = x.shape[0]
    rm = mod4.shape[2]
    return pl.pallas_call(
        _in_proj_kernel,
        grid=(t // tm, MAIN_COLS // COLT),
        in_specs=[pl.BlockSpec((tm, D_MODEL), lambda i, j: (i, 0)),
                  pl.BlockSpec((1, D_MODEL), lambda i, j: (0, 0)),
                  _mod_spec(0, rm, tpb), _mod_spec(1, rm, tpb),
                  pl.BlockSpec((None, D_MODEL, COLT), lambda i, j: (l, 0, jnp.minimum(j, N_HEAD_TILES - 1))),
                  pl.BlockSpec((D_MODEL, COLT), lambda i, j: (0, jnp.maximum(j - N_HEAD_TILES, 0))),
                  pl.BlockSpec((D_MODEL, LANES), lambda i, j: (0, 0))],
        out_specs=[pl.BlockSpec((tm, COLT), lambda i, j: (i, j)),
                   pl.BlockSpec((tm, LANES), lambda i, j: (i, 0))],
        out_shape=[jax.ShapeDtypeStruct((t, MAIN_COLS), F32), jax.ShapeDtypeStruct((t, LANES), F32)],
        scratch_shapes=[pltpu.VMEM((tm, D_MODEL), BF16)],
        compiler_params=_cparams(2),
        name="in_proj",
    )(x, norm_w, mod4, mod4, w_in, w_tail, w_dt)


def _s5_kernel(u_ref, wb_ref, wc_ref, ab_ref, d_ref, h0_ref, y_ref, hl_ref, bu_scr, hs_scr, st_scr, ub_scr, ut_scr,
               *, nb, tq, time_major):
    ns = S5_BLOCK_STATE
    rows = nb * tq
    tt = pl.program_id(1)

    @pl.when(tt == 0)
    def _():
        st_scr[...] = h0_ref[...]

    if time_major:
        u_tm = u_ref[...]
    else:
        ub_scr[...] = u_ref[...].reshape(rows, LANES)
        for s in range(tq):
            ut_scr[s * nb:(s + 1) * nb, :] = ub_scr[pl.ds(s, nb, stride=tq), :]
        u_tm = ut_scr[...]
    bu_scr[...] = _dot(u_tm.astype(BF16), wb_ref[...])
    ar = jnp.broadcast_to(ab_ref[:, :ns], (nb, ns))
    ai = jnp.broadcast_to(ab_ref[:, ns:], (nb, ns))

    def step(s, carry):
        hr, hi = carry
        r0 = pl.multiple_of(s * nb, nb)
        nhr = ar * hr - ai * hi + bu_scr[pl.ds(r0, nb), :ns]
        nhi = ar * hi + ai * hr + bu_scr[pl.ds(r0, nb), ns:]
        hs_scr[pl.ds(r0, nb), :ns] = nhr
        hs_scr[pl.ds(r0, nb), ns:] = nhi
        return nhr, nhi

    hr, hi = lax.fori_loop(0, tq, step, (st_scr[:, :ns], st_scr[:, ns:]), unroll=min(tq, 8))
    st_scr[:, :ns] = hr
    st_scr[:, ns:] = hi
    y_tm = _dot(hs_scr[...].astype(BF16), wc_ref[...])
    if time_major:
        y_ref[...] = _gelu_tanh(y_tm + d_ref[...] * u_tm)
    else:
        ut_scr[...] = y_tm
        for b in range(nb):
            y_b = ut_scr[pl.ds(b, tq, stride=nb), :] + d_ref[...] * ub_scr[b * tq:(b + 1) * tq, :]
            y_ref[b] = _gelu_tanh(y_b)

    @pl.when(tt == pl.num_programs(1) - 1)
    def _():
        hl_ref[...] = st_scr[...]


def _s5_branch(main, wb, wc, ab, d_skip, h0, nb, seq, time_major):
    t = main.shape[0]
    cb = S5_COLBLOCKS
    ucol = COL_U // LANES
    tq = min(seq, 512 // nb)
    rows = tq * nb
    if time_major:
        u_in, u_spec = main, pl.BlockSpec((rows, LANES), lambda j, i: (i, ucol + j))
        y_shape, y_spec = (t, D_MODEL), pl.BlockSpec((rows, LANES), lambda j, i: (i, j))
    else:
        u_in, u_spec = main.reshape(nb, seq, MAIN_COLS), pl.BlockSpec((nb, tq, LANES), lambda j, i: (0, i, ucol + j))
        y_shape, y_spec = (nb, seq, D_MODEL), pl.BlockSpec((nb, tq, LANES), lambda j, i: (0, i, j))
    y, hl = pl.pallas_call(
        functools.partial(_s5_kernel, nb=nb, tq=tq, time_major=time_major),
        grid=(cb, seq // tq),
        in_specs=[u_spec,
                  pl.BlockSpec((None, LANES, 2 * S5_BLOCK_STATE), lambda j, i: (j, 0, 0)),
                  pl.BlockSpec((None, 2 * S5_BLOCK_STATE, LANES), lambda j, i: (j, 0, 0)),
                  pl.BlockSpec((None, 1, 2 * S5_BLOCK_STATE), lambda j, i: (j, 0, 0)),
                  pl.BlockSpec((1, LANES), lambda j, i: (0, j)),
                  pl.BlockSpec((None, nb, 2 * S5_BLOCK_STATE), lambda j, i: (j, 0, 0))],
        out_specs=[y_spec, pl.BlockSpec((None, nb, 2 * S5_BLOCK_STATE), lambda j, i: (j, 0, 0))],
        out_shape=[jax.ShapeDtypeStruct(y_shape, F32),
                   jax.ShapeDtypeStruct((cb, nb, 2 * S5_BLOCK_STATE), F32)],
        scratch_shapes=[pltpu.VMEM((rows, 2 * S5_BLOCK_STATE), F32),
                        pltpu.VMEM((rows, 2 * S5_BLOCK_STATE), F32),
                        pltpu.VMEM((nb, 2 * S5_BLOCK_STATE), F32),
                        pltpu.VMEM((rows, LANES), F32),
                        pltpu.VMEM((rows, LANES), F32)],
        compiler_params=_cparams(2),
        name="s5_branch",
    )(u_in, wb, wc, ab, d_skip, h0)
    return y.reshape(t, D_MODEL), hl


def _ssd_kernel(x0_ref, x1_ref, bc_ref, z0_ref, z1_ref, dt_ref, conv0_ref, h0_ref, cw_ref, cb_ref, dtb_ref,
                alog_ref, dsk_ref, nw_ref, tril_ref, y_ref, hout_ref, convout_ref,
                xp_scr, xbc_scr, dt_scr, y_scr, xw_scr, st_scr, *, lr):
    q = M_CHUNK
    c = pl.program_id(1)
    last = pl.num_programs(1) - 1
    hp = M_HEADDIM
    gw = M_INNER // M_GROUPS

    @pl.when(c == 0)
    def _():
        st_scr[...] = h0_ref[...]
        xp_scr[0:SUBLANES, :] = conv0_ref[...]
        if lr < q:
            xp_scr[SUBLANES:, :] = jnp.zeros((q, M_CONV_DIM), F32)
            dt_scr[...] = jnp.zeros((q, LANES), F32)

    xp_scr[SUBLANES:SUBLANES + lr, 0:COLT] = x0_ref[...]
    xp_scr[SUBLANES:SUBLANES + lr, COLT:2 * COLT] = x1_ref[...]
    xp_scr[SUBLANES:SUBLANES + lr, 2 * COLT:3 * COLT] = bc_ref[...]
    acc = cb_ref[...] + cw_ref[0:1, :] * xp_scr[5:5 + q, :]
    for k in range(1, M_CONV):
        acc = acc + cw_ref[k:k + 1, :] * xp_scr[5 + k:5 + k + q, :]
    xbc_scr[...] = _silu(acc)
    xbc = xbc_scr

    win = ((lr + 5) // SUBLANES) * SUBLANES

    @pl.when(c == last)
    def _():
        convout_ref[...] = xp_scr[win:win + SUBLANES, :]

    if lr == q:
        xp_scr[0:SUBLANES, :] = xp_scr[q:q + SUBLANES, :]

    dt_scr[0:lr, :] = _softplus(dt_ref[...] + dtb_ref[...])
    dt = dt_scr[...]
    da = dt * (-jnp.exp(alog_ref[...]))
    tril = tril_ref[...]
    p3 = _split3(da)
    acum = _dot(tril, p3[0]) + _dot(tril, p3[1]) + _dot(tril, p3[2])
    acum_t = acum.T

    row_i = lax.broadcasted_iota(I32, (q, q), 0)
    col_j = lax.broadcasted_iota(I32, (q, q), 1)
    causal = row_i >= col_j
    low_half = lax.broadcasted_iota(I32, (q, LANES), 1) < hp

    for g in range(M_GROUPS):
        bm = xbc[:, M_INNER + g * M_DSTATE:M_INNER + (g + 1) * M_DSTATE].astype(BF16)
        cm = xbc[:, M_INNER + M_GROUPS * M_DSTATE + g * M_DSTATE:
                 M_INNER + M_GROUPS * M_DSTATE + (g + 1) * M_DSTATE].astype(BF16)
        cbm = lax.dot_general(cm, bm, (((1,), (1,)), ((), ())), preferred_element_type=F32)
        st_g = st_scr[g * gw:(g + 1) * gw, :]
        y_off = lax.dot_general(cm, st_g.astype(BF16), (((1,), (1,)), ((), ())),
                                preferred_element_type=F32)
        for kk in range(gw // LANES):
            k = g * (gw // LANES) + kk
            cols = slice(k * LANES, (k + 1) * LANES)
            h_a, h_b = 2 * k, 2 * k + 1
            col_a = jnp.broadcast_to(acum[:, h_a:h_a + 1], (q, LANES))
            col_b = jnp.broadcast_to(acum[:, h_b:h_b + 1], (q, LANES))
            dt_p = jnp.where(low_half, jnp.broadcast_to(dt[:, h_a:h_a + 1], (q, LANES)),
                             jnp.broadcast_to(dt[:, h_b:h_b + 1], (q, LANES)))
            ac_p = jnp.where(low_half, col_a, col_b)
            end_p = jnp.broadcast_to(ac_p[q - 1:q, :], (q, LANES))
            xs_p = xbc[:, cols]
            xdt = xs_p * dt_p
            xdt_b = xdt.astype(BF16)
            l_a = jnp.where(causal, jnp.exp(jnp.minimum(col_a - acum_t[h_a:h_a + 1, :], 0.0)), 0.0)
            l_b = jnp.where(causal, jnp.exp(jnp.minimum(col_b - acum_t[h_b:h_b + 1, :], 0.0)), 0.0)
            r_a = _dot((cbm * l_a).astype(BF16), xdt_b)
            r_b = _dot((cbm * l_b).astype(BF16), xdt_b)
            y_scr[:, cols] = (jnp.where(low_half, r_a, r_b) + y_off[:, kk * LANES:(kk + 1) * LANES] * jnp.exp(ac_p)
                              + dsk_ref[:, cols] * xs_p)
            xw_scr[:, cols] = xdt * jnp.exp(end_p - ac_p)
            dec = jnp.exp(ac_p[q - 1:q, :])
            dec_a = jnp.broadcast_to(dec[:, 0:1], (hp, M_DSTATE))
            dec_b = jnp.broadcast_to(dec[:, hp:hp + 1], (hp, M_DSTATE))
            st_scr[h_a * hp:(h_a + 1) * hp, :] = st_scr[h_a * hp:(h_a + 1) * hp, :] * dec_a
            st_scr[h_b * hp:(h_b + 1) * hp, :] = st_scr[h_b * hp:(h_b + 1) * hp, :] * dec_b
        xw_g = xw_scr[:, g * gw:(g + 1) * gw].astype(BF16)
        st_scr[g * gw:(g + 1) * gw, :] += lax.dot_general(xw_g, bm, (((0,), (0,)), ((), ())),
                                                          preferred_element_type=F32)

    for g in range(M_GROUPS):
        z_ref = z0_ref if g < M_GROUPS // 2 else z1_ref
        zc = (g % (M_GROUPS // 2)) * gw
        v = y_scr[0:lr, g * gw:(g + 1) * gw] * _silu(z_ref[:, zc:zc + gw])
        y_ref[:, g * gw:(g + 1) * gw] = (v * lax.rsqrt(jnp.mean(v * v, axis=-1, keepdims=True) + EPS)
                                         * nw_ref[:, g * gw:(g + 1) * gw])

    @pl.when(c == last)
    def _():
        hout_ref[...] = st_scr[...]


SHORT_Q = 16


def _ssd_short_kernel(x0_ref, x1_ref, bc_ref, z0_ref, z1_ref, dt_ref, conv0_ref, h0_ref, cw_ref, cb_ref, dtb_ref,
                      alog_ref, dsk_ref, nw_ref, y_ref, hout_ref, convout_ref, xp_scr, dt_scr, *, lr):
    q = SHORT_Q
    hp = M_HEADDIM
    gw = M_INNER // M_GROUPS
    xp_scr[0:SUBLANES, :] = conv0_ref[...]
    xp_scr[SUBLANES:, :] = jnp.zeros((q, M_CONV_DIM), F32)
    xp_scr[SUBLANES:SUBLANES + lr, 0:COLT] = x0_ref[...]
    xp_scr[SUBLANES:SUBLANES + lr, COLT:2 * COLT] = x1_ref[...]
    xp_scr[SUBLANES:SUBLANES + lr, 2 * COLT:3 * COLT] = bc_ref[...]
    acc = cb_ref[...] + cw_ref[0:1, :] * xp_scr[5:5 + q, :]
    for k in range(1, M_CONV):
        acc = acc + cw_ref[k:k + 1, :] * xp_scr[5 + k:5 + k + q, :]
    xbc = _silu(acc)
    win = ((lr + 5) // SUBLANES) * SUBLANES
    convout_ref[...] = xp_scr[win:win + SUBLANES, :]

    dt_scr[...] = jnp.zeros((q, LANES), F32)
    dt_scr[0:lr, :] = _softplus(dt_ref[...] + dtb_ref[...])
    dt = dt_scr[...]
    acum = dt * (-jnp.exp(alog_ref[...]))
    row = lax.broadcasted_iota(I32, (q, LANES), 0)
    shift = 1
    while shift < q:
        acum = acum + jnp.where(row >= shift, pltpu.roll(acum, shift, 0), 0.0)
        shift *= 2

    low_half = lax.broadcasted_iota(I32, (q, LANES), 1) < hp

    def per_lane(v):
        parts = []
        for k in range(M_INNER // LANES):
            a = jnp.broadcast_to(v[:, 2 * k:2 * k + 1], (q, LANES))
            b = jnp.broadcast_to(v[:, 2 * k + 1:2 * k + 2], (q, LANES))
            parts.append(jnp.where(low_half, a, b))
        return jnp.concatenate(parts, axis=-1)

    acx = per_lane(acum)
    xs = xbc[:, :M_INNER]
    xdt = xs * per_lane(dt)
    end = jnp.broadcast_to(acx[q - 1:q, :], (q, M_INNER))
    xw = (xdt * jnp.exp(end - acx)).astype(BF16)
    rows = lax.broadcasted_iota(I32, (q, M_INNER), 0)

    y = dsk_ref[...] * xs
    y_off, cbx = [], [[] for _ in range(lr)]
    for g in range(M_GROUPS):
        bm = xbc[:, M_INNER + g * M_DSTATE:M_INNER + (g + 1) * M_DSTATE].astype(BF16)
        cm = xbc[:, M_INNER + M_GROUPS * M_DSTATE + g * M_DSTATE:
                 M_INNER + M_GROUPS * M_DSTATE + (g + 1) * M_DSTATE].astype(BF16)
        st_g = h0_ref[g * gw:(g + 1) * gw, :]
        y_off.append(lax.dot_general(cm, st_g.astype(BF16), (((1,), (1,)), ((), ())), preferred_element_type=F32))
        for j in range(lr):
            cb_j = jnp.sum(cm.astype(F32) * bm[j:j + 1, :].astype(F32), axis=-1, keepdims=True)
            cbx[j].append(jnp.broadcast_to(cb_j, (q, gw)))
        new = lax.dot_general(xw[:, g * gw:(g + 1) * gw], bm, (((0,), (0,)), ((), ())), preferred_element_type=F32)
        for hh in range(gw // hp):
            h = g * (gw // hp) + hh
            dec = jnp.broadcast_to(jnp.exp(acum[q - 1:q, h:h + 1]), (hp, M_DSTATE))
            hout_ref[h * hp:(h + 1) * hp, :] = h0_ref[h * hp:(h + 1) * hp, :] * dec + new[hh * hp:(hh + 1) * hp, :]
    y = y + jnp.concatenate(y_off, axis=-1) * jnp.exp(acx)
    for j in range(lr):
        decay = jnp.where(rows >= j, jnp.exp(jnp.minimum(acx - acx[j:j + 1, :], 0.0)), 0.0)
        y = y + jnp.concatenate(cbx[j], axis=-1) * decay * xdt[j:j + 1, :]

    for g in range(M_GROUPS):
        z_ref = z0_ref if g < M_GROUPS // 2 else z1_ref
        zc = (g % (M_GROUPS // 2)) * gw
        v = y[0:lr, g * gw:(g + 1) * gw] * _silu(z_ref[:, zc:zc + gw])
        y_ref[:, g * gw:(g + 1) * gw] = (v * lax.rsqrt(jnp.mean(v * v, axis=-1, keepdims=True) + EPS)
                                         * nw_ref[:, g * gw:(g + 1) * gw])


def _ssd_short_branch(main_v, dt_v, conv0, h0, l, cw, cb, dtb, alog, dsk, nw, nb, seq):
    nct = MAIN_COLS // COLT
    tile = lambda k: pl.BlockSpec((seq, COLT), lambda b: (0, nct * b + k))
    const = lambda shape: pl.BlockSpec(shape, lambda b: (0,) * len(shape))
    kx, kz = COL_XBC // COLT, COL_Z // COLT
    return pl.pallas_call(
        functools.partial(_ssd_short_kernel, lr=seq),
        grid=(nb,),
        in_specs=[tile(kx), tile(kx + 1), tile(kx + 2), tile(kz), tile(kz + 1),
                  pl.BlockSpec((seq, LANES), lambda b: (0, b)),
                  pl.BlockSpec((None, SUBLANES, M_CONV_DIM), lambda b: (b, 0, 0)),
                  pl.BlockSpec((None, None, M_INNER, M_DSTATE), lambda b: (l, b, 0, 0)),
                  const((SUBLANES, M_CONV_DIM)), const((1, M_CONV_DIM)), const((1, LANES)), const((1, LANES)),
                  const((1, M_INNER)), const((1, M_INNER))],
        out_specs=[pl.BlockSpec((seq, M_INNER), lambda b: (0, b)),
                   pl.BlockSpec((None, M_INNER, M_DSTATE), lambda b: (b, 0, 0)),
                   pl.BlockSpec((None, SUBLANES, M_CONV_DIM), lambda b: (b, 0, 0))],
        out_shape=[jax.ShapeDtypeStruct((seq, nb * M_INNER), F32),
                   jax.ShapeDtypeStruct((nb, M_INNER, M_DSTATE), F32),
                   jax.ShapeDtypeStruct((nb, SUBLANES, M_CONV_DIM), F32)],
        scratch_shapes=[pltpu.VMEM((SHORT_Q + SUBLANES, M_CONV_DIM), F32), pltpu.VMEM((SHORT_Q, LANES), F32)],
        compiler_params=_cparams(1),
        name="ssd_short",
    )(main_v, main_v, main_v, main_v, main_v, dt_v, conv0, h0, cw, cb, dtb, alog, dsk, nw)


def _ssd_branch(main, dt_raw, conv0, h0, l, cw, cb, dtb, alog, dsk, nw, tril, nb, seq, time_major):
    q = M_CHUNK
    lr = min(q, seq)
    nc = seq // lr
    nct = MAIN_COLS // COLT
    if time_major and seq <= SUBLANES:
        y, hout, convout = _ssd_short_branch(main.reshape(seq, nb * MAIN_COLS), dt_raw.reshape(seq, nb * LANES),
                                             conv0, h0, l, cw, cb, dtb, alog, dsk, nw, nb, seq)
        off = lr + 5 - ((lr + 5) // SUBLANES) * SUBLANES
        return y.reshape(seq * nb, M_INNER), hout, convout[:, off:off + M_CONV - 1]
    if time_major:
        main_v, dt_v = main.reshape(seq, nb * MAIN_COLS), dt_raw.reshape(seq, nb * LANES)
        rowcol = lambda b, c, k: (c, nct * b + k)
        dt_map = lambda b, c: (c, b)
        y_shape, y_map = (seq, nb * M_INNER), (lambda b, c: (c, b))
    else:
        main_v, dt_v = main, dt_raw
        rowcol = lambda b, c, k: (b * nc + c, k)
        dt_map = lambda b, c: (b * nc + c, 0)
        y_shape, y_map = (seq * nb, M_INNER), (lambda b, c: (b * nc + c, 0))
    tile = lambda k: pl.BlockSpec((lr, COLT), lambda b, c: rowcol(b, c, k))
    const = lambda shape: pl.BlockSpec(shape, lambda b, c: (0,) * len(shape))
    kx, kz = COL_XBC // COLT, COL_Z // COLT
    y, hout, convout = pl.pallas_call(
        functools.partial(_ssd_kernel, lr=lr),
        grid=(nb, nc),
        in_specs=[tile(kx), tile(kx + 1), tile(kx + 2), tile(kz), tile(kz + 1),
                  pl.BlockSpec((lr, LANES), dt_map),
                  pl.BlockSpec((None, SUBLANES, M_CONV_DIM), lambda b, c: (b, 0, 0)),
                  pl.BlockSpec((None, None, M_INNER, M_DSTATE), lambda b, c: (l, b, 0, 0)),
                  const((SUBLANES, M_CONV_DIM)), const((1, M_CONV_DIM)), const((1, LANES)), const((1, LANES)),
                  const((1, M_INNER)), const((1, M_INNER)), const((q, q))],
        out_specs=[pl.BlockSpec((lr, M_INNER), y_map),
                   pl.BlockSpec((None, M_INNER, M_DSTATE), lambda b, c: (b, 0, 0)),
                   pl.BlockSpec((None, SUBLANES, M_CONV_DIM), lambda b, c: (b, 0, 0))],
        out_shape=[jax.ShapeDtypeStruct(y_shape, F32),
                   jax.ShapeDtypeStruct((nb, M_INNER, M_DSTATE), F32),
                   jax.ShapeDtypeStruct((nb, SUBLANES, M_CONV_DIM), F32)],
        scratch_shapes=[pltpu.VMEM((q + SUBLANES, M_CONV_DIM), F32),
                        pltpu.VMEM((q, M_CONV_DIM), F32),
                        pltpu.VMEM((q, LANES), F32),
                        pltpu.VMEM((q, M_INNER), F32),
                        pltpu.VMEM((q, M_INNER), F32),
                        pltpu.VMEM((M_INNER, M_DSTATE), F32)],
        compiler_params=_cparams(2),
        name="ssd_branch",
    )(main_v, main_v, main_v, main_v, main_v, dt_v, conv0, h0, cw, cb, dtb, alog, dsk, nw, tril)
    off = lr + 5 - ((lr + 5) // SUBLANES) * SUBLANES
    return y.reshape(seq * nb, M_INNER), hout, convout[:, off:off + M_CONV - 1]


def _route(logits, cnt_scr, ltri):
    tm = logits.shape[0]
    lane = lax.broadcasted_iota(I32, (tm, LANES), 1).astype(F32)
    neg = jnp.float32(-jnp.inf)
    far = jnp.float32(LANES)
    glog = jnp.where(lane < E_GROUPS, logits, neg)
    gmax = jnp.max(glog, axis=-1, keepdims=True)
    g_idx = jnp.min(jnp.where(glog == gmax, lane, far), axis=-1, keepdims=True)
    g_p = 1.0 / jnp.sum(jnp.exp(glog - gmax), axis=-1, keepdims=True)
    e_lane = lane - E_GROUPS
    in_group = (e_lane >= g_idx * E_PER_GROUP) & (e_lane < (g_idx + 1.0) * E_PER_GROUP)
    elog = jnp.where(in_group, logits, neg)
    v0 = jnp.max(elog, axis=-1, keepdims=True)
    i0 = jnp.min(jnp.where(elog == v0, lane, far), axis=-1, keepdims=True)
    elog1 = jnp.where(lane == i0, neg, elog)
    v1 = jnp.max(elog1, axis=-1, keepdims=True)
    i1 = jnp.min(jnp.where(elog1 == v1, lane, far), axis=-1, keepdims=True)
    e1x = jnp.exp(v1 - v0)
    w0 = g_p / (1.0 + e1x)
    w1 = g_p * e1x / (1.0 + e1x)
    e0, e1 = i0 - E_GROUPS, i1 - E_GROUPS
    hit0, hit1 = lane == e0, lane == e1
    onehot = jnp.where(hit0 | hit1, 1.0, 0.0)
    before = _dot(ltri, onehot.astype(BF16)) + cnt_scr[...]
    rank0 = jnp.sum(jnp.where(hit0, before, 0.0), axis=-1, keepdims=True)
    rank1 = jnp.sum(jnp.where(hit1, before, 0.0), axis=-1, keepdims=True)
    cnt_scr[...] += jnp.sum(onehot, axis=0, keepdims=True)
    ri = jnp.where(lane == 0, e0, jnp.where(lane == 1, e1, jnp.where(lane == 2, rank0, jnp.where(lane == 3, rank1, 0.0))))
    rw = jnp.where(lane == 0, w0, jnp.where(lane == 1, w1, 0.0))
    return ri, rw


def _merge_kernel(ya_ref, yb_ref, ga_ref, gb_ref, x_ref, g1_ref, sh2_ref, sc2_ref, n2_ref,
                  wglu_ref, wpa_ref, wpb_ref, wout_ref, wr_ref, br_ref, ltri_ref,
                  xo_ref, h2_ref, ri_ref, rw_ref, cnt_ref, cnt_scr):
    i = pl.program_id(0)

    @pl.when(i == 0)
    def _():
        cnt_scr[...] = jnp.zeros_like(cnt_scr)

    ya = ya_ref[...]
    ya = ya * _sigmoid(_dot(ya.astype(BF16), wglu_ref[...]))
    pa = _dot(ya.astype(BF16), wpa_ref[...])
    pb = _dot(yb_ref[...].astype(BF16), wpb_ref[...])
    merged = _sigmoid(ga_ref[...]) * pa + _sigmoid(gb_ref[...]) * pb
    x = x_ref[...] + g1_ref[...] * _dot(merged.astype(BF16), wout_ref[...])
    xo_ref[...] = x
    xn = x * lax.rsqrt(jnp.mean(x * x, axis=-1, keepdims=True) + EPS) * n2_ref[...]
    h2 = xn * (1.0 + sc2_ref[...]) + sh2_ref[...]
    h2_ref[...] = h2
    logits = _dot(h2.astype(BF16), wr_ref[...]) + br_ref[...]
    ri, rw = _route(logits, cnt_scr, ltri_ref[...])
    ri_ref[...] = ri.T[0:SUBLANES, :].astype(I32)
    rw_ref[...] = rw

    @pl.when(i == pl.num_programs(0) - 1)
    def _():
        cnt_ref[...] = cnt_scr[...]


def _merge(ya, yb, main, x, mod4, tpb, n2, wglu, wpa, wpb, wout, wr, br, ltri, tm):
    t = x.shape[0]
    rm = mod4.shape[2]
    row = lambda w: pl.BlockSpec((tm, w), lambda i: (i, 0))
    const = lambda shape: pl.BlockSpec(shape, lambda i: (0,) * len(shape), pipeline_mode=pl.Buffered(1))
    return pl.pallas_call(
        _merge_kernel,
        grid=(t // tm,),
        in_specs=[row(D_MODEL), row(M_INNER),
                  pl.BlockSpec((tm, D_MODEL), lambda i: (i, COL_GA // D_MODEL)),
                  pl.BlockSpec((tm, D_MODEL), lambda i: (i, COL_GB // D_MODEL)),
                  row(D_MODEL), _mod_spec(2, rm, tpb), _mod_spec(3, rm, tpb), _mod_spec(4, rm, tpb),
                  const((1, D_MODEL)),
                  const((D_MODEL, D_MODEL)), const((D_MODEL, D_MODEL)), const((M_INNER, D_MODEL)),
                  const((D_MODEL, D_MODEL)), const((D_MODEL, LANES)), const((1, LANES)), const((tm, tm))],
        out_specs=[row(D_MODEL), row(D_MODEL), pl.BlockSpec((SUBLANES, tm), lambda i: (0, i)), row(LANES),
                   pl.BlockSpec((1, LANES), lambda i: (0, 0))],
        out_shape=[jax.ShapeDtypeStruct((t, D_MODEL), F32), jax.ShapeDtypeStruct((t, D_MODEL), F32),
                   jax.ShapeDtypeStruct((SUBLANES, t), I32), jax.ShapeDtypeStruct((t, LANES), F32),
                   jax.ShapeDtypeStruct((1, LANES), F32)],
        scratch_shapes=[pltpu.VMEM((1, LANES), F32)],
        compiler_params=_cparams(1),
        name="merge_route",
    )(ya, yb, main, main, x, mod4, mod4, mod4, n2, wglu, wpa, wpb, wout, wr, br, ltri)


def _row_copy(src_ref, src_row, dst_ref, dst_row, sem):
    return pltpu.make_async_copy(src_ref.at[pl.ds(src_row, 1)], dst_ref.at[pl.ds(dst_row, 1)], sem)


def _dispatch_kernel(p0_ref, p1_ref, h2_ref, buf_ref, xs_ref, sem, *, tm):
    del buf_ref
    base = pl.program_id(0) * tm

    def issue(r, carry):
        _row_copy(h2_ref, r, xs_ref, p0_ref[base + r], sem).start()
        _row_copy(h2_ref, r, xs_ref, p1_ref[base + r], sem).start()
        return carry

    lax.fori_loop(0, tm, issue, 0, unroll=DMA_UNROLL)

    def drain(r, carry):
        _row_copy(h2_ref, r, xs_ref, p0_ref[base + r], sem).wait()
        _row_copy(h2_ref, r, xs_ref, p1_ref[base + r], sem).wait()
        return carry

    lax.fori_loop(0, tm, drain, 0, unroll=DMA_UNROLL)


def _dispatch(h2, pos0, pos1, buf, tm):
    t = h2.shape[0]
    return pl.pallas_call(
        functools.partial(_dispatch_kernel, tm=tm),
        grid_spec=pltpu.PrefetchScalarGridSpec(
            num_scalar_prefetch=2,
            grid=(t // tm,),
            in_specs=[pl.BlockSpec((tm, D_MODEL), lambda i, p0, p1: (i, 0)),
                      pl.BlockSpec(memory_space=pl.ANY)],
            out_specs=pl.BlockSpec(memory_space=pl.ANY),
            scratch_shapes=[pltpu.SemaphoreType.DMA(())]),
        out_shape=jax.ShapeDtypeStruct(buf.shape, F32),
        input_output_aliases={3: 0},
        compiler_params=_cparams(1, disable_bounds_checks=True),
        name="moe_dispatch",
    )(pos0, pos1, h2, buf)


def _expert_kernel(te_ref, tv_ref, xs_ref, wg_ref, wu_ref, wd_ref, o_ref, wg_scr, wu_scr, wd_scr):
    i = pl.program_id(0)

    @pl.when(tv_ref[i] == 2)
    def _():
        wg_scr[...] = wg_ref[...].astype(BF16)
        wu_scr[...] = wu_ref[...].astype(BF16)
        wd_scr[...] = wd_ref[...].astype(BF16)

    @pl.when(tv_ref[i] > 0)
    def _():
        xb = xs_ref[...].astype(BF16)
        act = _silu(_dot(xb, wg_scr[...])) * _dot(xb, wu_scr[...])
        o_ref[...] = _dot(act.astype(BF16), wd_scr[...])

    @pl.when(tv_ref[i] == 0)
    def _():
        o_ref[...] = jnp.zeros_like(o_ref)


def _experts(xs, tile_expert, tile_valid, wg, wu, wd, l):
    npad = xs.shape[0]
    te = EXPERT_TILE
    return pl.pallas_call(
        _expert_kernel,
        grid_spec=pltpu.PrefetchScalarGridSpec(
            num_scalar_prefetch=2,
            grid=(npad // te,),
            in_specs=[pl.BlockSpec((te, D_MODEL), lambda i, e, v: (i, 0)),
                      pl.BlockSpec((None, None, D_MODEL, E_FF), lambda i, e, v: (l, e[i], 0, 0)),
                      pl.BlockSpec((None, None, D_MODEL, E_FF), lambda i, e, v: (l, e[i], 0, 0)),
                      pl.BlockSpec((None, None, E_FF, D_MODEL), lambda i, e, v: (l, e[i], 0, 0))],
            out_specs=pl.BlockSpec((te, D_MODEL), lambda i, e, v: (i, 0)),
            scratch_shapes=[pltpu.VMEM((D_MODEL, E_FF), BF16), pltpu.VMEM((D_MODEL, E_FF), BF16),
                            pltpu.VMEM((E_FF, D_MODEL), BF16)]),
        out_shape=jax.ShapeDtypeStruct((npad, D_MODEL), F32),
        compiler_params=_cparams(1),
        name="moe_experts",
    )(tile_expert, tile_valid, xs, wg, wu, wd)


def _combine_kernel(p0_ref, p1_ref, ys_ref, x_ref, rw_ref, g2_ref, nf_ref, o_ref, b0, b1, sem, *, tm, final):
    base = pl.program_id(0) * tm

    def issue(r, carry):
        _row_copy(ys_ref, p0_ref[base + r], b0, r, sem).start()
        _row_copy(ys_ref, p1_ref[base + r], b1, r, sem).start()
        return carry

    lax.fori_loop(0, tm, issue, 0, unroll=DMA_UNROLL)

    def drain(r, carry):
        _row_copy(ys_ref, p0_ref[base + r], b0, r, sem).wait()
        _row_copy(ys_ref, p1_ref[base + r], b1, r, sem).wait()
        return carry

    lax.fori_loop(0, tm, drain, 0, unroll=DMA_UNROLL)
    rw = rw_ref[...]
    w0 = jnp.broadcast_to(rw[:, 0:1], (tm, D_MODEL))
    w1 = jnp.broadcast_to(rw[:, 1:2], (tm, D_MODEL))
    x = x_ref[...] + g2_ref[...] * (w0 * b0[...] + w1 * b1[...])
    if final:
        x = x * lax.rsqrt(jnp.mean(x * x, axis=-1, keepdims=True) + EPS) * nf_ref[...]
    o_ref[...] = x


def _combine(ys, pos0, pos1, x, rw, mod4, tpb, norm_f, tm, final):
    t = x.shape[0]
    rm = mod4.shape[2]
    row = lambda w: pl.BlockSpec((tm, w), lambda i, p0, p1: (i, 0))
    return pl.pallas_call(
        functools.partial(_combine_kernel, tm=tm, final=final),
        grid_spec=pltpu.PrefetchScalarGridSpec(
            num_scalar_prefetch=2,
            grid=(t // tm,),
            in_specs=[pl.BlockSpec(memory_space=pl.ANY), row(D_MODEL), row(LANES),
                      _mod_spec(5, rm, tpb),
                      pl.BlockSpec((1, D_MODEL), lambda i, p0, p1: (0, 0))],
            out_specs=row(D_MODEL),
            scratch_shapes=[pltpu.VMEM((tm, D_MODEL), F32), pltpu.VMEM((tm, D_MODEL), F32),
                            pltpu.SemaphoreType.DMA(())]),
        out_shape=jax.ShapeDtypeStruct((t, D_MODEL), F32),
        compiler_params=_cparams(1, disable_bounds_checks=True),
        name="moe_combine",
    )(pos0, pos1, ys, x, rw, mod4, norm_f)


def _moe_plan(routes, counts, n_rows):
    te = EXPERT_TILE
    experts = jnp.arange(N_EXPERTS, dtype=I32)
    cnts = [c[0, :N_EXPERTS].astype(I32) for c in counts]
    total = sum(cnts)
    padded = ((total + te - 1) // te) * te
    ends = jnp.cumsum(padded)
    offs = ends - padded
    pos = []
    for ri in routes:
        lookup = lambda e: jnp.sum(jnp.where(e[:, None] == experts[None, :], offs[None, :], 0), axis=1)
        pos.append(((lookup(ri[0]) + ri[2]).astype(I32), (lookup(ri[1]) + ri[3]).astype(I32)))
        offs = offs + cnts[len(pos) - 1]
    starts = jnp.arange(n_rows // te, dtype=I32) * te
    tile_expert = jnp.minimum(jnp.sum((starts[:, None] >= ends[None, :]).astype(I32), axis=1), N_EXPERTS - 1)
    first = jnp.any(starts[:, None] == (ends - padded)[None, :], axis=1)
    tile_valid = jnp.where(starts < ends[-1], 1 + first.astype(I32), 0).astype(I32)
    return pos, tile_expert, tile_valid


class _Group:
    def __init__(self, nb, seq, time_major):
        self.nb, self.seq, self.time_major = nb, seq, time_major
        self.t = nb * seq
        span = self.t if time_major else seq
        self.tm = min(512, span)
        self.tm_in = min(2048, span)
        self.tm_merge = min(512, span)

    def mod4(self, mod_l, tm):
        m = mod_l.reshape(self.nb, 6, D_MODEL).transpose(1, 0, 2)
        if self.time_major:
            return jnp.tile(m, (1, tm // self.nb, 1))[:, None], self.t // tm
        return m[:, :, None, :], self.seq // tm


def _mixers(g, xt, mod_l, s5_re0, s5_im0, ssm_all, conv0, p, l):
    nb, seq, cb = g.nb, g.seq, S5_COLBLOCKS
    mod_in, tpb_in = g.mod4(mod_l, g.tm_in)
    main, dt_raw = _in_proj(xt, p['norm1'][l], mod_in, tpb_in, p['w_in'], l, p['w_tail'][l], p['w_dt'][l], g.tm_in)
    h0 = jnp.concatenate([s5_re0.reshape(nb, cb, S5_BLOCK_STATE), s5_im0.reshape(nb, cb, S5_BLOCK_STATE)],
                         axis=-1).transpose(1, 0, 2)
    ya, hl = _s5_branch(main, p['wb'][l], p['wc'][l], p['ab'][l], p['s5_d'][l], h0, nb, seq, g.time_major)
    hl = hl.transpose(1, 0, 2)
    new_re = hl[..., :S5_BLOCK_STATE].reshape(nb, S5_GROUPS, S5_STATE)
    new_im = hl[..., S5_BLOCK_STATE:].reshape(nb, S5_GROUPS, S5_STATE)
    conv_in = jnp.pad(conv0, ((0, 0), (SUBLANES - (M_CONV - 1), 0), (0, 0)))
    yb, h_ssm, cbuf = _ssd_branch(main, dt_raw, conv_in, ssm_all.reshape(-1, nb, M_INNER, M_DSTATE), l,
                                  p['conv_w'][l], p['conv_b'][l], p['dt_bias'][l], p['a_log'][l],
                                  p['m_d'][l], p['m_norm'][l], p['tril'], nb, seq, g.time_major)
    mod_mg, tpb_mg = g.mod4(mod_l, g.tm_merge)
    xt, h2, ri, rw, counts = _merge(ya, yb, main, xt, mod_mg, tpb_mg, p['norm2'][l], p['w_glu'][l],
                                    p['w_pa'][l], p['w_pb'][l], p['w_out'][l], p['w_r'][l], p['b_r'][l],
                                    p['ltri'][:g.tm_merge, :g.tm_merge], g.tm_merge)
    states = (new_re, new_im, h_ssm.reshape(nb, M_HEADS, M_HEADDIM, M_DSTATE), cbuf)
    mod_cb, tpb_cb = g.mod4(mod_l, g.tm)
    return xt, h2, ri, rw, counts, mod_cb, tpb_cb, states


def _prepare(norm1, norm2, w_in, s5_a_re, s5_a_im, s5_log_dt, s5_b_re, s5_b_im, s5_c_re, s5_c_im, s5_d, w_glu,
             conv_w, conv_b, dt_bias, a_log, m_d, m_norm, w_pa, w_pb, w_out, router_g, router_g_b, router_e,
             router_e_b, w_gate, w_up, w_down, norm_f):
    c_dt = COL_GA
    w_dt = jnp.pad(w_in[..., c_dt:c_dt + M_HEADS], ((0, 0), (0, 0), (0, LANES - M_HEADS))).astype(BF16)
    w_tail = w_in[..., c_dt + M_HEADS:].astype(BF16)
    ab_re, ab_im, bb_re, bb_im = _s5_discretise(s5_a_re, s5_a_im, s5_log_dt, s5_b_re, s5_b_im)
    wb, wc, ab = _s5_weights(ab_re, ab_im, bb_re, bb_im, s5_c_re, s5_c_im)
    pad_heads = lambda v: jnp.pad(v, ((0, 0), (0, LANES - M_HEADS))).reshape(DEPTH, 1, LANES)
    w_r = jnp.pad(jnp.concatenate([router_g, router_e], axis=-1),
                  ((0, 0), (0, 0), (0, LANES - E_GROUPS - N_EXPERTS))).astype(BF16)
    b_r = jnp.pad(jnp.concatenate([router_g_b, router_e_b], axis=-1),
                  ((0, 0), (0, LANES - E_GROUPS - N_EXPERTS))).reshape(DEPTH, 1, LANES)
    idx = jnp.arange(512)
    return dict(
        norm1=norm1.reshape(DEPTH, 1, D_MODEL), norm2=norm2.reshape(DEPTH, 1, D_MODEL),
        norm_f=norm_f.reshape(1, D_MODEL), w_in=w_in.astype(BF16), w_tail=w_tail, w_dt=w_dt, wb=wb, wc=wc, ab=ab,
        s5_d=s5_d.reshape(DEPTH, 1, D_MODEL), w_glu=w_glu.astype(BF16),
        conv_w=jnp.pad(conv_w, ((0, 0), (0, SUBLANES - M_CONV), (0, 0))),
        conv_b=conv_b.reshape(DEPTH, 1, M_CONV_DIM), dt_bias=pad_heads(dt_bias), a_log=pad_heads(a_log),
        m_d=jnp.repeat(m_d, M_HEADDIM, axis=-1).reshape(DEPTH, 1, M_INNER), m_norm=m_norm.reshape(DEPTH, 1, M_INNER),
        w_pa=w_pa.astype(BF16), w_pb=w_pb.astype(BF16), w_out=w_out.astype(BF16), w_r=w_r, b_r=b_r,
        w_gate=w_gate.reshape(DEPTH, N_EXPERTS, D_MODEL, E_FF),
        w_up=w_up.reshape(DEPTH, N_EXPERTS, D_MODEL, E_FF),
        w_down=w_down.reshape(DEPTH, N_EXPERTS, E_FF, D_MODEL),
        tril=(idx[:M_CHUNK, None] >= idx[None, :M_CHUNK]).astype(BF16),
        ltri=(idx[:, None] > idx[None, :]).astype(BF16),
    )


def _forward(xs, mods, states, groups, p):
    n_rows = 2 * sum(g.t for g in groups) + N_EXPERTS * EXPERT_TILE
    new_states = [[] for _ in groups]
    for l in range(DEPTH):
        mixed = [_mixers(g, x, m[l], st[0][l], st[1][l], st[2], st[3][l], p, l)
                 for g, x, m, st in zip(groups, xs, mods, states)]
        pos, tile_expert, tile_valid = _moe_plan([m[2] for m in mixed], [m[4] for m in mixed], n_rows)
        buf = jnp.zeros((n_rows, D_MODEL), F32)
        for g, m, (p0, p1) in zip(groups, mixed, pos):
            buf = _dispatch(m[1], p0, p1, buf, g.tm)
        ys = _experts(buf, tile_expert, tile_valid, p['w_gate'], p['w_up'], p['w_down'], l)
        xs = [_combine(ys, p0, p1, m[0], m[3], m[5], m[6], p['norm_f'], g.tm, l == DEPTH - 1)
              for g, m, (p0, p1) in zip(groups, mixed, pos)]
        for k, m in enumerate(mixed):
            new_states[k].append(m[7])
    stacked = [tuple(jnp.stack([layer[i] for layer in ns]) for i in range(4)) for ns in new_states]
    return xs, stacked


def kernel(x_prompt, x_sample, state_s5_re, state_s5_im, state_ssm, state_conv, c_prompt, c_sample, norm1, norm2, w_mod, b_mod, w_in, s5_a_re, s5_a_im, s5_log_dt, s5_b_re, s5_b_im, s5_c_re, s5_c_im, s5_d, w_glu, conv_w, conv_b, dt_bias, a_log, m_d, m_norm, w_pa, w_pb, w_out, router_g, router_g_b, router_e, router_e_b, w_gate, w_up, w_down, norm_f):
    p = _prepare(norm1, norm2, w_in, s5_a_re, s5_a_im, s5_log_dt, s5_b_re, s5_b_im, s5_c_re, s5_c_im, s5_d, w_glu,
                 conv_w, conv_b, dt_bias, a_log, m_d, m_norm, w_pa, w_pb, w_out, router_g, router_g_b, router_e,
                 router_e_b, w_gate, w_up, w_down, norm_f)
    bp, lp, _ = x_prompt.shape
    bs, ls, _ = x_sample.shape
    groups = [_Group(bp, lp, time_major=False), _Group(bs, ls, time_major=True)]
    mod = _modulation(jnp.concatenate([c_prompt, c_sample], axis=0), w_mod, b_mod)
    zero = lambda s: jnp.zeros((DEPTH, bp) + s.shape[2:], s.dtype)
    states = [(zero(state_s5_re), zero(state_s5_im), zero(state_ssm), zero(state_conv)),
              (state_s5_re, state_s5_im, state_ssm, state_conv)]
    xs = [x_prompt.reshape(bp * lp, D_MODEL), jnp.swapaxes(x_sample, 0, 1).reshape(bs * ls, D_MODEL)]
    (y_p, y_s), (st_p, st_s) = _forward(xs, [mod[:, :bp], mod[:, bp:]], states, groups, p)
    y_p = y_p.reshape(bp, lp, D_MODEL)
    y_s = jnp.swapaxes(y_s.reshape(ls, bs, D_MODEL), 0, 1)
    return (y_p, y_s) + st_p + st_s
```
